```python
import math
import jax, jax.numpy as jnp
from jax import lax
import numpy as np

D_MODEL = 2048
BATCH = 2
SEQ = 16384
DEPTH = 4
DEC_BATCH = 8
DEC_SEQ = 32
PAST_LEN = 1024

CHUNK = 64
N_META = 16
D_CONV = D_MODEL // 2
D_SSM = D_MODEL // 2
D_MIX = D_CONV + D_SSM
CONV_WIDTH = 3
SSM_GROUP = 16
N_SSM_GROUPS = D_SSM // SSM_GROUP
SSM_STATE = 64
D_FF = 5632
D_IN_PROJ = 3 * D_CONV + D_SSM
EPS = 1e-6
DT_MIN = 1e-3
DT_MAX = 1e-1

kernel_name = "hymba_conv_s5_macaron_stream"


def rmsnorm(x, g):
    xf = x.astype(jnp.float32)
    y = xf * lax.rsqrt(jnp.mean(xf * xf, axis=-1, keepdims=True) + EPS)
    return (y * g.astype(jnp.float32)).astype(x.dtype)


def swiglu(x, w_gate, w_up, w_down):
    return (jax.nn.silu(x @ w_gate) * (x @ w_up)) @ w_down


def short_conv(b_gate, c_gate, h, conv_buf, conv_w):
    v = c_gate * h
    vp = jnp.concatenate([conv_buf.astype(v.dtype), v], axis=1)
    t = v.shape[1]
    w = conv_w.astype(v.dtype)
    y = w[0] * vp[:, :t] + w[1] * vp[:, 1:t + 1] + w[2] * vp[:, 2:]
    return b_gate * y, vp[:, t:]


def s5_ssm(u, h0_re, h0_im, lam_re, lam_im, log_dt, b_re, b_im, c_re, c_im, d_skip, glu_w):
    f32 = jnp.float32
    bsz, t, _ = u.shape
    lam = lax.complex(lam_re.astype(f32), lam_im.astype(f32))
    dt = jnp.exp(log_dt.astype(f32))[:, None]
    a_bar = jnp.exp(lam * dt)
    b = lax.complex(b_re.astype(f32), b_im.astype(f32))
    b_bar = ((a_bar - 1.0) / lam)[..., None] * b
    c = lax.complex(c_re.astype(f32), c_im.astype(f32))
    ug = u.astype(f32).reshape(bsz, t, N_SSM_GROUPS, SSM_GROUP)
    bu = jnp.einsum('btgh,gph->btgp', ug.astype(jnp.complex64), b_bar)
    h0 = lax.complex(h0_re.astype(f32), h0_im.astype(f32))
    bu = bu.at[:, 0].add(a_bar * h0)
    a_seq = jnp.broadcast_to(a_bar, bu.shape)

    def combine(left, right):
        a_l, b_l = left
        a_r, b_r = right
        return a_r * a_l, a_r * b_l + b_r

    _, hs = lax.associative_scan(combine, (a_seq, bu), axis=1)
    y = jnp.einsum('btgp,ghp->btgh', hs, c).real + d_skip.astype(f32) * ug
    y = jax.nn.gelu(y.reshape(bsz, t, D_SSM))
    y = y * jax.nn.sigmoid(y @ glu_w.astype(f32))
    h_last = hs[:, -1]
    return y.astype(u.dtype), h_last.real.astype(h0_re.dtype), h_last.imag.astype(h0_re.dtype)


def trunk_layer(x, conv_buf, h_re, h_im, lp):
    x = x + 0.5 * swiglu(rmsnorm(x, lp['ffn1_norm']), lp['ffn1_w_gate'], lp['ffn1_w_up'], lp['ffn1_w_down'])
    xn = rmsnorm(x, lp['mix_norm'])
    proj = xn @ lp['w_in']
    b_gate, c_gate, h_conv, u = jnp.split(proj, [D_CONV, 2 * D_CONV, 3 * D_CONV], axis=-1)
    y_conv, new_buf = short_conv(b_gate, c_gate, h_conv, conv_buf, lp['conv_w'])
    y_ssm, new_re, new_im = s5_ssm(u, h_re, h_im, lp['ssm_lambda_re'], lp['ssm_lambda_im'],
                                   lp['ssm_log_dt'], lp['ssm_b_re'], lp['ssm_b_im'],
                                   lp['ssm_c_re'], lp['ssm_c_im'], lp['ssm_d'], lp['ssm_glu_w'])
    mixed = jnp.concatenate([rmsnorm(y_conv, lp['conv_out_norm']),
                             rmsnorm(y_ssm, lp['ssm_out_norm'])], axis=-1)
    x = x + mixed @ lp['w_out']
    x = x + 0.5 * swiglu(rmsnorm(x, lp['ffn2_norm']), lp['ffn2_w_gate'], lp['ffn2_w_up'], lp['ffn2_w_down'])
    return x, new_buf, new_re, new_im


def setup_inputs(seed: int = 0) -> dict:
    key = jax.random.key(seed)
    ks = jax.random.split(key, 40)
    f32 = jnp.float32
    nrm = lambda k, shape, scale: scale * jax.random.normal(k, shape, f32)
    gain = lambda k, shape: 1.0 + 0.05 * jax.random.normal(k, shape, f32)
    G, P, H, L = N_SSM_GROUPS, SSM_STATE, SSM_GROUP, DEPTH
    lam_im_base = jnp.broadcast_to(jnp.pi * jnp.arange(P, dtype=f32), (L, G, P))
    return {
        'x_prompt': nrm(ks[0], (BATCH, SEQ, D_MODEL), 1.0),
        'x_sample': nrm(ks[1], (DEC_BATCH, DEC_SEQ, D_MODEL), 1.0),
        'cache_conv': nrm(ks[2], (L, DEC_BATCH, CONV_WIDTH - 1, D_CONV), 0.5),
        'state_ssm_re': nrm(ks[3], (L, DEC_BATCH, G, P), 0.5),
        'state_ssm_im': nrm(ks[4], (L, DEC_BATCH, G, P), 0.5),
        'meta_tokens': nrm(ks[5], (N_META, D_MODEL), 1.0),
        'ffn1_norm': gain(ks[6], (L, D_MODEL)),
        'ffn1_w_gate': nrm(ks[7], (L, D_MODEL, D_FF), D_MODEL ** -0.5),
        'ffn1_w_up': nrm(ks[8], (L, D_MODEL, D_FF), D_MODEL ** -0.5),
        'ffn1_w_down': nrm(ks[9], (L, D_FF, D_MODEL), D_FF ** -0.5),
        'mix_norm': gain(ks[10], (L, D_MODEL)),
        'w_in': nrm(ks[11], (L, D_MODEL, D_IN_PROJ), D_MODEL ** -0.5),
        'conv_w': nrm(ks[12], (L, CONV_WIDTH, D_CONV), CONV_WIDTH ** -0.5),
        'conv_out_norm': gain(ks[13], (L, D_CONV)),
        'ssm_lambda_re': -0.5 + 0.01 * jax.random.normal(ks[14], (L, G, P), f32),
        'ssm_lambda_im': lam_im_base + 0.01 * jax.random.normal(ks[15], (L, G, P), f32),
        'ssm_log_dt': jax.random.uniform(ks[16], (L, G), f32, math.log(DT_MIN), math.log(DT_MAX)),
        'ssm_b_re': nrm(ks[17], (L, G, P, H), (2.0 * H) ** -0.5),
        'ssm_b_im': nrm(ks[18], (L, G, P, H), (2.0 * H) ** -0.5),
        'ssm_c_re': nrm(ks[19], (L, G, H, P), (2.0 * P) ** -0.5),
        'ssm_c_im': nrm(ks[20], (L, G, H, P), (2.0 * P) ** -0.5),
        'ssm_d': nrm(ks[21], (L, G, H), 1.0),
        'ssm_glu_w': nrm(ks[22], (L, D_SSM, D_SSM), D_SSM ** -0.5),
        'ssm_out_norm': gain(ks[23], (L, D_SSM)),
        'w_out': nrm(ks[24], (L, D_MIX, D_MODEL), D_MIX ** -0.5),
        'ffn2_norm': gain(ks[25], (L, D_MODEL)),
        'ffn2_w_gate': nrm(ks[26], (L, D_MODEL, D_FF), D_MODEL ** -0.5),
        'ffn2_w_up': nrm(ks[27], (L, D_MODEL, D_FF), D_MODEL ** -0.5),
        'ffn2_w_down': nrm(ks[28], (L, D_FF, D_MODEL), D_FF ** -0.5),
        'final_norm': gain(ks[29], (D_MODEL,)),
    }


def reference(x_prompt, x_sample, cache_conv, state_ssm_re, state_ssm_im, meta_tokens,
              ffn1_norm, ffn1_w_gate, ffn1_w_up, ffn1_w_down,
              mix_norm, w_in, conv_w, conv_out_norm,
              ssm_lambda_re, ssm_lambda_im, ssm_log_dt, ssm_b_re, ssm_b_im,
              ssm_c_re, ssm_c_im, ssm_d, ssm_glu_w, ssm_out_norm, w_out,
              ffn2_norm, ffn2_w_gate, ffn2_w_up, ffn2_w_down, final_norm):
    bsz = x_prompt.shape[0]
    meta = jnp.broadcast_to(meta_tokens.astype(x_prompt.dtype)[None], (bsz, N_META, D_MODEL))
    xp = jnp.concatenate([meta, x_prompt], axis=1)
    xs = x_sample
    zero_buf = jnp.zeros((bsz, CONV_WIDTH - 1, D_CONV), xp.dtype)
    zero_h = jnp.zeros((bsz, N_SSM_GROUPS, SSM_STATE), state_ssm_re.dtype)

    conv_p, re_p, im_p, conv_s, re_s, im_s = [], [], [], [], [], []
    for l in range(DEPTH):
        lp = {
            'ffn1_norm': ffn1_norm[l], 'ffn1_w_gate': ffn1_w_gate[l], 'ffn1_w_up': ffn1_w_up[l],
            'ffn1_w_down': ffn1_w_down[l], 'mix_norm': mix_norm[l], 'w_in': w_in[l],
            'conv_w': conv_w[l], 'conv_out_norm': conv_out_norm[l],
            'ssm_lambda_re': ssm_lambda_re[l], 'ssm_lambda_im': ssm_lambda_im[l],
            'ssm_log_dt': ssm_log_dt[l], 'ssm_b_re': ssm_b_re[l], 'ssm_b_im': ssm_b_im[l],
            'ssm_c_re': ssm_c_re[l], 'ssm_c_im': ssm_c_im[l], 'ssm_d': ssm_d[l],
            'ssm_glu_w': ssm_glu_w[l], 'ssm_out_norm': ssm_out_norm[l], 'w_out': w_out[l],
            'ffn2_norm': ffn2_norm[l], 'ffn2_w_gate': ffn2_w_gate[l], 'ffn2_w_up': ffn2_w_up[l],
            'ffn2_w_down': ffn2_w_down[l],
        }
        xp, b_p, r_p, i_p = trunk_layer(xp, zero_buf, zero_h, zero_h, lp)
        xs, b_s, r_s, i_s = trunk_layer(xs, cache_conv[l], state_ssm_re[l], state_ssm_im[l], lp)
        conv_p.append(b_p); re_p.append(r_p); im_p.append(i_p)
        conv_s.append(b_s); re_s.append(r_s); im_s.append(i_s)

    y_prompt = rmsnorm(xp, final_norm)[:, N_META:]
    y_sample = rmsnorm(xs, final_norm)
    new_conv_prompt = jnp.stack(conv_p)
    new_ssm_re_prompt = jnp.stack(re_p)
    new_ssm_im_prompt = jnp.stack(im_p)
    new_conv_sample = jnp.stack(conv_s)
    new_ssm_re_sample = jnp.stack(re_s)
    new_ssm_im_sample = jnp.stack(im_s)
    return (y_prompt, y_sample, new_conv_prompt, new_ssm_re_prompt, new_ssm_im_prompt,
            new_conv_sample, new_ssm_re_sample, new_ssm_im_sample)
```

```python
import functools

import jax
import jax.numpy as jnp
from jax import lax
from jax.experimental import pallas as pl
from jax.experimental.pallas import tpu as pltpu

F32 = jnp.float32
BF16 = jnp.bfloat16
EPS = 1e-6

V7X_LANES = 128
V7X_SUBLANES = 8
V7X_MXU_DIM = 256
V7X_VMEM_LIMIT_BYTES = 56 * 1024 * 1024

ROW_TILE = 512
FF_TILE = 512
SSM_GROUPS_PER_STEP = 4
PREP_GROUPS_PER_STEP = 8


def _row_tile(m, target):
    if m <= target:
        return m
    t = (target // V7X_SUBLANES) * V7X_SUBLANES
    while t > V7X_SUBLANES and m % t:
        t -= V7X_SUBLANES
    assert m % t == 0, (m, target)
    return t


def _lane_tile(n, target):
    t = (min(n, target) // V7X_LANES) * V7X_LANES
    while t > V7X_LANES and n % t:
        t -= V7X_LANES
    assert t > 0 and n % t == 0, (n, target)
    return t


def _params(*sem):
    return pltpu.CompilerParams(dimension_semantics=sem, vmem_limit_bytes=V7X_VMEM_LIMIT_BYTES)


def _resident(block_shape, index_map):
    return pl.BlockSpec(block_shape, index_map, pipeline_mode=pl.Buffered(1))


def _rmsnorm(x, g):
    ms = jnp.mean(x * x, axis=-1, keepdims=True)
    return x * lax.rsqrt(ms + EPS) * g


def _dot(a, b):
    return jnp.dot(a, b, preferred_element_type=F32)


def _dot_nt(a, b):
    return lax.dot_general(a, b, (((1,), (1,)), ((), ())), preferred_element_type=F32)


def _ffn_kernel(x_ref, g_ref, wg_ref, wu_ref, wd_ref, o_ref, xn_ref, acc_ref):
    j = pl.program_id(1)

    @pl.when(j == 0)
    def _():
        xn_ref[...] = _rmsnorm(x_ref[...], g_ref[...]).astype(BF16)
        acc_ref[...] = jnp.zeros_like(acc_ref)

    xn = xn_ref[...]
    hg = _dot(xn, wg_ref[...])
    hu = _dot(xn, wu_ref[...])
    h = (jax.nn.silu(hg) * hu).astype(BF16)
    acc_ref[...] += _dot(h, wd_ref[...])

    @pl.when(j == pl.num_programs(1) - 1)
    def _():
        o_ref[...] = x_ref[...] + 0.5 * acc_ref[...]


def _ffn(x, g, wg, wu, wd):
    m, d = x.shape
    f = wg.shape[1]
    tm = _row_tile(m, ROW_TILE)
    tf = _lane_tile(f, FF_TILE)
    return pl.pallas_call(
        _ffn_kernel,
        grid=(m // tm, f // tf),
        in_specs=[
            pl.BlockSpec((tm, d), lambda i, j: (i, 0)),
            pl.BlockSpec((1, d), lambda i, j: (0, 0)),
            pl.BlockSpec((d, tf), lambda i, j: (0, j)),
            pl.BlockSpec((d, tf), lambda i, j: (0, j)),
            pl.BlockSpec((tf, d), lambda i, j: (j, 0)),
        ],
        out_specs=pl.BlockSpec((tm, d), lambda i, j: (i, 0)),
        out_shape=jax.ShapeDtypeStruct((m, d), F32),
        scratch_shapes=[pltpu.VMEM((tm, d), BF16), pltpu.VMEM((tm, d), F32)],
        compiler_params=_params("parallel", "arbitrary"),
        name="ffn",
    )(x, g.reshape(1, d), wg, wu, wd)


def _inproj_kernel(x_ref, g_ref, w_ref, cw_ref, cg_ref, cinit_ref,
                   yc_ref, u_ref, cnew_ref, carry_ref, *, ns, ts, blocks_per_seq, dc):
    i = pl.program_id(0)
    xn = _rmsnorm(x_ref[...], g_ref[...]).astype(BF16)
    b_gate = _dot(xn, w_ref[:, 0:dc])
    v = _dot(xn, w_ref[:, dc:2 * dc]) * _dot(xn, w_ref[:, 2 * dc:3 * dc])
    u_ref[...] = _dot(xn, w_ref[:, 3 * dc:])

    if blocks_per_seq > 1:
        @pl.when((i % blocks_per_seq) == 0)
        def _():
            carry_ref[0:2, :] = cinit_ref[0]

    row = lax.broadcasted_iota(jnp.int32, (ts, 1), 0)
    w0, w1, w2 = cw_ref[0:1, :], cw_ref[1:2, :], cw_ref[2:3, :]
    for s in range(ns):
        vs = v[s * ts:(s + 1) * ts]
        prev = carry_ref if blocks_per_seq > 1 else cinit_ref.at[s]
        p0, p1 = prev[0:1, :], prev[1:2, :]
        vm1 = jnp.where(row == 0, p1, pltpu.roll(vs, 1, 0))
        vm2 = jnp.where(row == 0, p0, jnp.where(row == 1, p1, pltpu.roll(vs, 2, 0)))
        y = b_gate[s * ts:(s + 1) * ts] * (w0 * vm2 + w1 * vm1 + w2 * vs)
        yc_ref[s * ts:(s + 1) * ts, :] = _rmsnorm(y, cg_ref[...]).astype(yc_ref.dtype)
        tail = vs[ts - 2:ts]
        cnew_ref[s] = tail
        if blocks_per_seq > 1:
            carry_ref[0:2, :] = tail


def _inproj(x, g, w_in, conv_w, conv_g, conv_init, seq_len):
    m, d = x.shape
    n_seq = m // seq_len
    dc = conv_w.shape[1]
    ds = w_in.shape[1] - 3 * dc
    if seq_len >= ROW_TILE:
        tm = _row_tile(seq_len, ROW_TILE)
        ns, ts, bps = 1, tm, seq_len // tm
    else:
        tm, ns, ts, bps = m, n_seq, seq_len, 1
    kern = functools.partial(_inproj_kernel, ns=ns, ts=ts, blocks_per_seq=bps, dc=dc)
    return pl.pallas_call(
        kern,
        grid=(m // tm,),
        in_specs=[
            pl.BlockSpec((tm, d), lambda i: (i, 0)),
            _resident((1, d), lambda i: (0, 0)),
            _resident(w_in.shape, lambda i: (0, 0)),
            _resident(conv_w.shape, lambda i: (0, 0)),
            _resident((1, dc), lambda i: (0, 0)),
            pl.BlockSpec((ns, 2, dc), lambda i: (i // bps, 0, 0)),
        ],
        out_specs=[
            pl.BlockSpec((tm, dc), lambda i: (i, 0)),
            pl.BlockSpec((tm, ds), lambda i: (i, 0)),
            pl.BlockSpec((ns, 2, dc), lambda i: (i // bps, 0, 0)),
        ],
        out_shape=[
            jax.ShapeDtypeStruct((m, dc), BF16),
            jax.ShapeDtypeStruct((m, ds), F32),
            jax.ShapeDtypeStruct((n_seq, 2, dc), F32),
        ],
        scratch_shapes=[pltpu.VMEM((V7X_SUBLANES, dc), F32)],
        compiler_params=_params("arbitrary"),
        name="inproj",
    )(x, g.reshape(1, d), w_in, conv_w, conv_g.reshape(1, dc), conv_init)


def _cmul(ar, ai, br, bi):
    return ar * br - ai * bi, ar * bi + ai * br


def _ssm_kernel(u_ref, toep_ref, bm_ref, cmt_ref, dt_ref, al_ref, init_ref,
                y_ref, fin_ref, sl_ref, e_ref, *, gb, n_seq, nc, scan_in_vregs):
    half = V7X_LANES // 2
    lane = lax.broadcasted_iota(jnp.int32, (1, V7X_LANES), 1)
    sgn = jnp.where(lane < half, -1.0, 1.0).astype(F32)

    def swap(x):
        return pltpu.roll(x, half, 1)

    for gi in range(gb):
        ub = u_ref[gi].astype(BF16)
        sl_ref[...] = _dot(ub, bm_ref[gi].astype(BF16))
        a1r, a1i = al_ref[gi, 0:1, :], al_ref[gi, 1:2, :]

        if scan_in_vregs:
            pw = [(jnp.ones_like(a1r), jnp.zeros_like(a1r)), (a1r, a1i)]
            for k in range(2, V7X_SUBLANES + 1):
                pw.append(_cmul(*pw[k // 2], *pw[k - k // 2]))
            row = lax.broadcasted_iota(jnp.int32, (V7X_SUBLANES, V7X_LANES), 0)
            entry_r = jnp.zeros((V7X_SUBLANES, V7X_LANES), F32)
            entry_i = jnp.zeros((V7X_SUBLANES, V7X_LANES), F32)
            for r in range(V7X_SUBLANES):
                entry_r = jnp.where(row == r, pw[r][0], entry_r)
                entry_i = jnp.where(row == r, sgn * pw[r][1], entry_i)
            levels = []
            for k in (1, 2, 4):
                levels.append((k, jnp.where(row >= k, pw[k][0], 0.0),
                               jnp.where(row >= k, sgn * pw[k][1], 0.0)))
            one_r = jnp.broadcast_to(a1r, (V7X_SUBLANES, V7X_LANES))
            one_i = jnp.broadcast_to(sgn * a1i, (V7X_SUBLANES, V7X_LANES))
            step_r = jnp.broadcast_to(pw[V7X_SUBLANES][0], (V7X_SUBLANES, V7X_LANES))
            step_i = jnp.broadcast_to(sgn * pw[V7X_SUBLANES][1], (V7X_SUBLANES, V7X_LANES))

            def body(k, carry):
                out = []
                for b in range(n_seq):
                    e0 = carry[b]
                    r0 = pl.multiple_of(b * nc + k * V7X_SUBLANES, V7X_SUBLANES)
                    sl = sl_ref[pl.ds(r0, V7X_SUBLANES), :]
                    w = jnp.where(row == 0, 0.0, pltpu.roll(sl, 1, 0))
                    for (sh, lr, li) in levels:
                        wk = pltpu.roll(w, sh, 0)
                        w = w + lr * wk + li * swap(wk)
                    e0s = swap(e0)
                    e_ref[pl.ds(r0, V7X_SUBLANES), :] = w + entry_r * e0 + entry_i * e0s
                    z = one_r * w + one_i * swap(w) + sl
                    z_last = jnp.broadcast_to(z[V7X_SUBLANES - 1:V7X_SUBLANES, :], z.shape)
                    out.append(step_r * e0 + step_i * e0s + z_last)
                return tuple(out)

            init = tuple(jnp.broadcast_to(init_ref[gi, b:b + 1, :], (V7X_SUBLANES, V7X_LANES))
                         for b in range(n_seq))
            last = lax.fori_loop(0, nc // V7X_SUBLANES, body, init)
            for b in range(n_seq):
                fin_ref[gi, b:b + 1, :] = last[b][0:1, :]
        else:
            e = init_ref[gi]
            for c in range(nc):
                e_ref[c * n_seq:(c + 1) * n_seq, :] = e
                e = a1r * e + (sgn * a1i) * swap(e) + sl_ref[c * n_seq:(c + 1) * n_seq, :]
            fin_ref[gi] = e

        y = _dot(ub, toep_ref[gi].astype(BF16))
        y = y + _dot_nt(e_ref[...].astype(BF16), cmt_ref[gi].astype(BF16))
        y = y + dt_ref[gi] * u_ref[gi]
        y_ref[gi] = jax.nn.gelu(y)


def _ssm(ug, tabs, layer, init, n_seq, nc):
    toep, bm, cmt, dt, al = tabs
    g, n, lh = ug.shape
    p2 = bm.shape[-1]
    scan_in_vregs = nc % V7X_SUBLANES == 0
    gb = SSM_GROUPS_PER_STEP if scan_in_vregs else min(g, 2 * SSM_GROUPS_PER_STEP)
    assert g % gb == 0
    kern = functools.partial(_ssm_kernel, gb=gb, n_seq=n_seq, nc=nc, scan_in_vregs=scan_in_vregs)
    tab_spec = lambda a: pl.BlockSpec((None, gb) + a.shape[2:], lambda i: (layer, i, 0, 0))
    return pl.pallas_call(
        kern,
        grid=(g // gb,),
        in_specs=[
            pl.BlockSpec((gb, n, lh), lambda i: (i, 0, 0)),
            tab_spec(toep), tab_spec(bm), tab_spec(cmt), tab_spec(dt), tab_spec(al),
            pl.BlockSpec((gb, n_seq, p2), lambda i: (i, 0, 0)),
        ],
        out_specs=[
            pl.BlockSpec((gb, n, lh), lambda i: (i, 0, 0)),
            pl.BlockSpec((gb, n_seq, p2), lambda i: (i, 0, 0)),
        ],
        out_shape=[
            jax.ShapeDtypeStruct((g, n, lh), F32),
            jax.ShapeDtypeStruct((g, n_seq, p2), F32),
        ],
        scratch_shapes=[pltpu.VMEM((n, p2), F32), pltpu.VMEM((n, p2), F32)],
        compiler_params=_params("parallel"),
        name="ssm",
    )(ug, toep, bm, cmt, dt, al, init)


def _ssm_prep_kernel(lr_ref, li_ref, ldt_ref, br_ref, bi_ref, cr_ref, ci_ref,
                     toep_ref, bm_ref, cmt_ref, al_ref, *, gp, chunk, h):
    half = V7X_LANES // 2
    lo = lax.broadcasted_iota(jnp.int32, (1, V7X_LANES), 1) < half
    lane_lh = lax.broadcasted_iota(jnp.int32, (1, chunk * h), 1)

    def one_group(gi, _):
        lr, li = lr_ref[gi], li_ref[gi]
        dt = jnp.exp(ldt_ref[gi])
        mag = jnp.exp(lr * dt)
        ar, ai = mag * jnp.cos(li * dt), mag * jnp.sin(li * dt)
        den = lr * lr + li * li
        qr = ((ar - 1.0) * lr + ai * li) / den
        qi = (ai * lr - (ar - 1.0) * li) / den
        bbr, bbi = _cmul(qr, qi, br_ref[gi], bi_ref[gi])
        cr, ci = cr_ref[gi], ci_ref[gi]

        pw = [(jnp.ones_like(ar), jnp.zeros_like(ar))]
        for _k in range(chunk):
            pw.append(_cmul(*pw[-1], ar, ai))
        al_ref[gi, 0:1, :] = pw[chunk][0]
        al_ref[gi, 1:2, :] = pw[chunk][1]

        for s in range(chunk):
            er, ei = _cmul(*pw[chunk - 1 - s], bbr, bbi)
            bm_ref[gi, s * h:(s + 1) * h, :] = jnp.where(lo, er, ei)
        cpow = []
        for k in range(chunk + 1):
            er, ei = _cmul(cr, ci, *pw[k])
            cpow.append(jnp.where(lo, er, -ei))
        for t in range(chunk):
            cmt_ref[gi, t * h:(t + 1) * h, :] = cpow[t + 1]
        gen = lax.dot_general(jnp.where(lo, bbr, bbi), jnp.concatenate(cpow[:chunk], axis=0),
                              (((1,), (1,)), ((), ())), preferred_element_type=F32,
                              precision=lax.Precision.HIGHEST)
        toep_ref[gi, 0:h, :] = gen
        for s in range(1, chunk):
            toep_ref[gi, s * h:(s + 1) * h, :] = jnp.where(
                lane_lh >= s * h, pltpu.roll(gen, s * h, 1), 0.0)
        return 0

    lax.fori_loop(0, gp, one_group, 0)


def _ssm_prep(lam_re, lam_im, log_dt, b_re, b_im, c_re, c_im, d_skip, chunk):
    nl, g, p = lam_re.shape
    h = d_skip.shape[-1]
    assert 2 * p == V7X_LANES and chunk * h == V7X_MXU_DIM
    twice = lambda a: jnp.concatenate([a, a], axis=-1)
    lr = twice(lam_re)[:, :, None, :]
    li = twice(lam_im)[:, :, None, :]
    ldt = jnp.broadcast_to(log_dt[:, :, None, None], (nl, g, 1, 2 * p))
    br = twice(jnp.swapaxes(b_re, -1, -2))
    bi = twice(jnp.swapaxes(b_im, -1, -2))
    cr, ci = twice(c_re), twice(c_im)
    gp = PREP_GROUPS_PER_STEP if g % PREP_GROUPS_PER_STEP == 0 else g
    spec = lambda r, c: pl.BlockSpec((None, gp, r, c), lambda l, i: (l, i, 0, 0))
    kern = functools.partial(_ssm_prep_kernel, gp=gp, chunk=chunk, h=h)
    lh = chunk * h
    toep, bm, cmt, al = pl.pallas_call(
        kern,
        grid=(nl, g // gp),
        in_specs=[spec(1, 2 * p)] * 3 + [spec(h, 2 * p)] * 4,
        out_specs=[spec(lh, lh), spec(lh, 2 * p), spec(lh, 2 * p), spec(2, 2 * p)],
        out_shape=[
            jax.ShapeDtypeStruct((nl, g, lh, lh), F32),
            jax.ShapeDtypeStruct((nl, g, lh, 2 * p), F32),
            jax.ShapeDtypeStruct((nl, g, lh, 2 * p), F32),
            jax.ShapeDtypeStruct((nl, g, 2, 2 * p), F32),
        ],
        compiler_params=_params("parallel", "parallel"),
        name="ssm_prep",
    )(lr, li, ldt, br, bi, cr, ci)
    dt = jnp.tile(d_skip, (1, 1, chunk))[:, :, None, :]
    return toep, bm, cmt, dt, al


def _mixout_kernel(x_ref, yc_ref, ys_ref, glu_ref, sg_ref, wo_ref, o_ref, *, dc):
    y = ys_ref[...]
    z = y * jax.nn.sigmoid(_dot(y.astype(BF16), glu_ref[...]))
    zn = _rmsnorm(z, sg_ref[...]).astype(BF16)
    o_ref[...] = x_ref[...] + _dot(yc_ref[...], wo_ref[0:dc, :]) + _dot(zn, wo_ref[dc:, :])


def _mixout(x, yc, ys, glu_w, ssm_g, w_out):
    m, d = x.shape
    dc, ds = yc.shape[1], ys.shape[1]
    tm = _row_tile(m, ROW_TILE)
    return pl.pallas_call(
        functools.partial(_mixout_kernel, dc=dc),
        grid=(m // tm,),
        in_specs=[
            pl.BlockSpec((tm, d), lambda i: (i, 0)),
            pl.BlockSpec((tm, dc), lambda i: (i, 0)),
            pl.BlockSpec((tm, ds), lambda i: (i, 0)),
            _resident(glu_w.shape, lambda i: (0, 0)),
            _resident((1, ds), lambda i: (0, 0)),
            _resident(w_out.shape, lambda i: (0, 0)),
        ],
        out_specs=pl.BlockSpec((tm, d), lambda i: (i, 0)),
        out_shape=jax.ShapeDtypeStruct((m, d), F32),
        compiler_params=_params("parallel"),
        name="mixout",
    )(x, yc, ys, glu_w, ssm_g.reshape(1, ds), w_out)


def _norm_kernel(x_ref, g_ref, o_ref):
    o_ref[...] = _rmsnorm(x_ref[...], g_ref[...])


def _final_norm(x, g):
    m, d = x.shape
    tm = _row_tile(m, ROW_TILE)
    return pl.pallas_call(
        _norm_kernel,
        grid=(m // tm,),
        in_specs=[pl.BlockSpec((tm, d), lambda i: (i, 0)), pl.BlockSpec((1, d), lambda i: (0, 0))],
        out_specs=pl.BlockSpec((tm, d), lambda i: (i, 0)),
        out_shape=jax.ShapeDtypeStruct((m, d), F32),
        compiler_params=_params("parallel"),
        name="final_norm",
    )(x, g.reshape(1, d))


def _trunk(x, seq_len, conv_init, ssm_init, wts, tabs, chunk):
    m = x.shape[0]
    n_seq = m // seq_len
    nc = seq_len // chunk
    assert seq_len % chunk == 0
    g = tabs[0].shape[1]
    hh = tabs[0].shape[2] // chunk
    seq_major = nc % V7X_SUBLANES == 0
    conv_out, ssm_out = [], []
    for l in range(len(conv_init)):
        w = wts[l]
        x = _ffn(x, w["ffn1_norm"], w["ffn1_w_gate"], w["ffn1_w_up"], w["ffn1_w_down"])
        yc, u, conv_new = _inproj(x, w["mix_norm"], w["w_in"], w["conv_w"], w["conv_out_norm"],
                                  conv_init[l], seq_len)
        u5 = u.reshape(n_seq, nc, chunk, g, hh)
        perm = (3, 0, 1, 2, 4) if seq_major else (3, 1, 0, 2, 4)
        ug = jnp.transpose(u5, perm).reshape(g, n_seq * nc, chunk * hh)
        yg, fin = _ssm(ug, tabs, l, ssm_init[l], n_seq, nc)
        y5 = yg.reshape((g, n_seq, nc, chunk, hh) if seq_major else (g, nc, n_seq, chunk, hh))
        inv = (1, 2, 3, 0, 4) if seq_major else (2, 1, 3, 0, 4)
        ys = jnp.transpose(y5, inv).reshape(m, g * hh)
        x = _mixout(x, yc, ys, w["ssm_glu_w"], w["ssm_out_norm"], w["w_out"])
        x = _ffn(x, w["ffn2_norm"], w["ffn2_w_gate"], w["ffn2_w_up"], w["ffn2_w_down"])
        conv_out.append(conv_new)
        ssm_out.append(fin)
    return x, conv_out, ssm_out


def kernel(x_prompt, x_sample, cache_conv, state_ssm_re, state_ssm_im, meta_tokens, ffn1_norm, ffn1_w_gate, ffn1_w_up, ffn1_w_down, mix_norm, w_in, conv_w, conv_out_norm, ssm_lambda_re, ssm_lambda_im, ssm_log_dt, ssm_b_re, ssm_b_im, ssm_c_re, ssm_c_im, ssm_d, ssm_glu_w, ssm_out_norm, w_out, ffn2_norm, ffn2_w_gate, ffn2_w_up, ffn2_w_down, final_norm):
    bsz, seq, d = x_prompt.shape
    dec_b, dec_t, _ = x_sample.shape
    n_meta = meta_tokens.shape[0]
    depth, g, p = ssm_lambda_re.shape
    h = ssm_d.shape[-1]
    dc = conv_w.shape[-1]
    chunk = V7X_MXU_DIM // h

    tabs = _ssm_prep(ssm_lambda_re, ssm_lambda_im, ssm_log_dt, ssm_b_re, ssm_b_im,
                     ssm_c_re, ssm_c_im, ssm_d, chunk)
    cast = lambda a: a.astype(BF16)
    wts = [dict(
        ffn1_norm=ffn1_norm[l], ffn1_w_gate=cast(ffn1_w_gate[l]), ffn1_w_up=cast(ffn1_w_up[l]),
        ffn1_w_down=cast(ffn1_w_down[l]), mix_norm=mix_norm[l], w_in=cast(w_in[l]),
        conv_w=conv_w[l], conv_out_norm=conv_out_norm[l], ssm_glu_w=cast(ssm_glu_w[l]),
        ssm_out_norm=ssm_out_norm[l], w_out=cast(w_out[l]), ffn2_norm=ffn2_norm[l],
        ffn2_w_gate=cast(ffn2_w_gate[l]), ffn2_w_up=cast(ffn2_w_up[l]),
        ffn2_w_down=cast(ffn2_w_down[l])) for l in range(depth)]

    meta_b = V7X_SUBLANES
    xm = jnp.tile(meta_tokens.astype(F32), (meta_b, 1))
    zc = [jnp.zeros((meta_b, 2, dc), F32)] * depth
    zs = [jnp.zeros((g, meta_b, 2 * p), F32)] * depth
    _, conv_m, ssm_m = _trunk(xm, n_meta, zc, zs, wts, tabs, chunk)

    s_init = jnp.transpose(jnp.concatenate([state_ssm_re, state_ssm_im], axis=-1), (0, 2, 1, 3))
    xs, conv_s, ssm_s = _trunk(x_sample.reshape(dec_b * dec_t, d), dec_t,
                               [cache_conv[l] for l in range(depth)],
                               [s_init[l] for l in range(depth)], wts, tabs, chunk)

    conv_p0 = [jnp.broadcast_to(c[0:1], (bsz, 2, dc)) for c in conv_m]
    ssm_p0 = [jnp.broadcast_to(s[:, 0:1], (g, bsz, 2 * p)) for s in ssm_m]
    xp, conv_p, ssm_p = _trunk(x_prompt.reshape(bsz * seq, d), seq, conv_p0, ssm_p0, wts, tabs, chunk)

    y_prompt = _final_norm(xp, final_norm).reshape(bsz, seq, d)
    y_sample = _final_norm(xs, final_norm).reshape(dec_b, dec_t, d)

    def states(fins):
        s = jnp.transpose(jnp.stack(fins), (0, 2, 1, 3))
        return s[..., :p], s[..., p:]

    re_p, im_p = states(ssm_p)
    re_s, im_s = states(ssm_s)
    return (y_prompt, y_sample, jnp.stack(conv_p), re_p, im_p, jnp.stack(conv_s), re_s, im_s)
```

```python
import functools

import jax
import jax.numpy as jnp
from jax import lax
from jax.experimental import pallas as pl
from jax.experimental.pallas import tpu as pltpu

F32 = jnp.float32
BF16 = jnp.bfloat16
EPS = 1e-6

V7X_LANES = 128
V7X_SUBLANES = 8
V7X_VMEM_LIMIT_BYTES = 56 * 1024 * 1024

ROW_TILE = 512
FF_TILE = 512
SSM_CHUNK = V7X_SUBLANES
SSM_SEG_TOKENS = 4096


def _row_tile(m, target, unit=V7X_SUBLANES):
    if m <= target:
        return m
    t = (target // unit) * unit
    while t > unit and m % t:
        t -= unit
    assert m % t == 0, (m, target)
    return t


def _lane_tile(n, target):
    t = (min(n, target) // V7X_LANES) * V7X_LANES
    while t > V7X_LANES and n % t:
        t -= V7X_LANES
    assert t > 0 and n % t == 0, (n, target)
    return t


def _params(*sem):
    return pltpu.CompilerParams(dimension_semantics=sem, vmem_limit_bytes=V7X_VMEM_LIMIT_BYTES)


def _resident(block_shape, index_map):
    return pl.BlockSpec(block_shape, index_map, pipeline_mode=pl.Buffered(1))


def _rmsnorm(x, g):
    ms = jnp.mean(x * x, axis=-1, keepdims=True)
    return x * lax.rsqrt(ms + EPS) * g


def _dot(a, b):
    return jnp.dot(a, b, preferred_element_type=F32)


def _dot_nt(a, b, precision=None):
    return lax.dot_general(a, b, (((1,), (1,)), ((), ())), preferred_element_type=F32,
                           precision=precision)


def _ffn_kernel(x_ref, g_ref, wg_ref, wu_ref, wd_ref, o_ref, xn_ref, acc_ref):
    j = pl.program_id(1)

    @pl.when(j == 0)
    def _():
        xn_ref[...] = _rmsnorm(x_ref[...], g_ref[...]).astype(BF16)
        acc_ref[...] = jnp.zeros_like(acc_ref)

    xn = xn_ref[...]
    hg = _dot(xn, wg_ref[...])
    hu = _dot(xn, wu_ref[...])
    h = (jax.nn.silu(hg) * hu).astype(BF16)
    acc_ref[...] += _dot(h, wd_ref[...])

    @pl.when(j == pl.num_programs(1) - 1)
    def _():
        o_ref[...] = x_ref[...] + 0.5 * acc_ref[...]


def _ffn(x, g, wg, wu, wd):
    m, d = x.shape
    f = wg.shape[1]
    tm = _row_tile(m, ROW_TILE)
    tf = _lane_tile(f, FF_TILE)
    return pl.pallas_call(
        _ffn_kernel,
        grid=(m // tm, f // tf),
        in_specs=[
            pl.BlockSpec((tm, d), lambda i, j: (i, 0)),
            pl.BlockSpec((1, d), lambda i, j: (0, 0)),
            pl.BlockSpec((d, tf), lambda i, j: (0, j)),
            pl.BlockSpec((d, tf), lambda i, j: (0, j)),
            pl.BlockSpec((tf, d), lambda i, j: (j, 0)),
        ],
        out_specs=pl.BlockSpec((tm, d), lambda i, j: (i, 0)),
        out_shape=jax.ShapeDtypeStruct((m, d), F32),
        scratch_shapes=[pltpu.VMEM((tm, d), BF16), pltpu.VMEM((tm, d), F32)],
        compiler_params=_params("parallel", "arbitrary"),
        name="ffn",
    )(x, g.reshape(1, d), wg, wu, wd)


def _inproj_kernel(x_ref, g_ref, w_ref, cw_ref, cg_ref, cinit_ref,
                   yc_ref, u_ref, cnew_ref, carry_ref, *, ns, ts, blocks_per_seq, dc):
    i = pl.program_id(0)
    xn = _rmsnorm(x_ref[...], g_ref[...]).astype(BF16)
    b_gate = _dot(xn, w_ref[:, 0:dc])
    v = _dot(xn, w_ref[:, dc:2 * dc]) * _dot(xn, w_ref[:, 2 * dc:3 * dc])
    u_ref[...] = _dot(xn, w_ref[:, 3 * dc:])

    if blocks_per_seq > 1:
        @pl.when((i % blocks_per_seq) == 0)
        def _():
            carry_ref[0:2, :] = cinit_ref[0]

    row = lax.broadcasted_iota(jnp.int32, (ts, 1), 0)
    w0, w1, w2 = cw_ref[0:1, :], cw_ref[1:2, :], cw_ref[2:3, :]
    for s in range(ns):
        vs = v[s * ts:(s + 1) * ts]
        prev = carry_ref if blocks_per_seq > 1 else cinit_ref.at[s]
        p0, p1 = prev[0:1, :], prev[1:2, :]
        vm1 = jnp.where(row == 0, p1, pltpu.roll(vs, 1, 0))
        vm2 = jnp.where(row == 0, p0, jnp.where(row == 1, p1, pltpu.roll(vs, 2, 0)))
        y = b_gate[s * ts:(s + 1) * ts] * (w0 * vm2 + w1 * vm1 + w2 * vs)
        yc_ref[s * ts:(s + 1) * ts, :] = _rmsnorm(y, cg_ref[...]).astype(yc_ref.dtype)
        tail = vs[ts - 2:ts]
        cnew_ref[s] = tail
        if blocks_per_seq > 1:
            carry_ref[0:2, :] = tail


def _inproj(x, g, w_in, conv_w, conv_g, conv_init, seq_len):
    m, d = x.shape
    n_seq = m // seq_len
    dc = conv_w.shape[1]
    ds = w_in.shape[1] - 3 * dc
    if seq_len >= ROW_TILE:
        tm = _row_tile(seq_len, ROW_TILE)
        ns, ts, bps = 1, tm, seq_len // tm
    else:
        tm, ns, ts, bps = m, n_seq, seq_len, 1
    kern = functools.partial(_inproj_kernel, ns=ns, ts=ts, blocks_per_seq=bps, dc=dc)
    return pl.pallas_call(
        kern,
        grid=(m // tm,),
        in_specs=[
            pl.BlockSpec((tm, d), lambda i: (i, 0)),
            _resident((1, d), lambda i: (0, 0)),
            _resident(w_in.shape, lambda i: (0, 0)),
            _resident(conv_w.shape, lambda i: (0, 0)),
            _resident((1, dc), lambda i: (0, 0)),
            pl.BlockSpec((ns, 2, dc), lambda i: (i // bps, 0, 0)),
        ],
        out_specs=[
            pl.BlockSpec((tm, dc), lambda i: (i, 0)),
            pl.BlockSpec((tm, ds), lambda i: (i, 0)),
            pl.BlockSpec((ns, 2, dc), lambda i: (i // bps, 0, 0)),
        ],
        out_shape=[
            jax.ShapeDtypeStruct((m, dc), BF16),
            jax.ShapeDtypeStruct((m, ds), F32),
            jax.ShapeDtypeStruct((n_seq, 2, dc), F32),
        ],
        scratch_shapes=[pltpu.VMEM((V7X_SUBLANES, dc), F32)],
        compiler_params=_params("arbitrary"),
        name="inproj",
    )(x, g.reshape(1, d), w_in, conv_w, conv_g.reshape(1, dc), conv_init)


def _cmul(ar, ai, br, bi):
    return ar * br - ai * bi, ar * bi + ai * br


def _swap_halves(x):
    return pltpu.roll(x, V7X_LANES // 2, 1)


def _scan_short(al_ref, init_ref, fin_ref, sl_ref, e_ref, *, gt, nb, nckb, sgn):
    for g in range(gt):
        a_r, a_i = al_ref[g, 0:1, :], sgn * al_ref[g, 1:2, :]
        e = init_ref[g]
        for c in range(nckb):
            rows = pl.ds(c, nb, stride=nckb)
            e_ref[g, rows, :] = e
            e = a_r * e + a_i * _swap_halves(e) + sl_ref[g, rows, :]
        fin_ref[g] = e


def _scan_long(al_ref, init_ref, fin_ref, sl_ref, e_ref, w_ref, z_ref, zs_ref, tab_ref, st_ref,
               *, gt, nck, sgn):
    n, sub = V7X_LANES, V7X_SUBLANES
    seg = pl.program_id(2)
    row_in_vreg = lax.broadcasted_iota(jnp.int32, (nck, 1), 0) & (sub - 1)
    row8 = lax.broadcasted_iota(jnp.int32, (sub, n), 0)
    for g in range(gt):
        a_r, a_i = al_ref[g, 0:1, :], al_ref[g, 1:2, :]
        pw = [(jnp.ones_like(a_r), jnp.zeros_like(a_r)), (a_r, a_i)]
        for k in range(2, sub + 1):
            pw.append(_cmul(*pw[k // 2], *pw[k - k // 2]))
        ent_r = jnp.zeros((sub, n), F32)
        ent_i = jnp.zeros((sub, n), F32)
        for r in range(sub):
            ent_r = jnp.where(row8 == r, pw[r][0], ent_r)
            ent_i = jnp.where(row8 == r, sgn * pw[r][1], ent_i)
        tab_ref[4 * g] = ent_r
        tab_ref[4 * g + 1] = ent_i
        tab_ref[4 * g + 2] = jnp.broadcast_to(pw[sub][0], (sub, n))
        tab_ref[4 * g + 3] = jnp.broadcast_to(sgn * pw[sub][1], (sub, n))

        @pl.when(seg == 0)
        def _():
            e0 = jnp.broadcast_to(init_ref[g:g + 1, :], (sub, n))
            st_ref[2 * g] = e0
            st_ref[2 * g + 1] = _swap_halves(e0)

        sl = sl_ref[g]
        w = jnp.where(row_in_vreg == 0, 0.0, pltpu.roll(sl, 1, 0))
        for k in (1, 2, 4):
            wk = pltpu.roll(w, k, 0)
            w = w + jnp.where(row_in_vreg >= k, pw[k][0] * wk + (sgn * pw[k][1]) * _swap_halves(wk), 0.0)
        z = a_r * w + (sgn * a_i) * _swap_halves(w) + sl
        w_ref[g] = w
        z_ref[g] = z
        zs_ref[g] = _swap_halves(z)

    def body(k, carry):
        r0 = pl.multiple_of(k * sub, sub)
        out = []
        for g in range(gt):
            e0, e0s = carry[2 * g], carry[2 * g + 1]
            z7 = jnp.broadcast_to(z_ref[g, pl.ds(r0 + sub - 1, 1), :], (sub, n))
            z7s = jnp.broadcast_to(zs_ref[g, pl.ds(r0 + sub - 1, 1), :], (sub, n))
            e_ref[g, pl.ds(r0, sub), :] = (w_ref[g, pl.ds(r0, sub), :]
                                           + tab_ref[4 * g] * e0 + tab_ref[4 * g + 1] * e0s)
            s_r, s_i = tab_ref[4 * g + 2], tab_ref[4 * g + 3]
            out.append(s_r * e0 + s_i * e0s + z7)
            out.append(s_r * e0s - s_i * e0 + z7s)
        return tuple(out)

    carry = lax.fori_loop(0, nck // sub, body, tuple(st_ref[i] for i in range(2 * gt)), unroll=2)
    for i in range(2 * gt):
        st_ref[i] = carry[i]

    @pl.when(seg == pl.num_programs(2) - 1)
    def _():
        for g in range(gt):
            fin_ref[g:g + 1, :] = carry[2 * g][0:1, :]


def _ssm_kernel(u_ref, toep_ref, bm_ref, cmt_ref, dcat_ref, al_ref, init_ref, y_ref, fin_ref,
                sl_ref, e_ref, *scan_scratch, chunk, gt, nb, nckb, long_scan):
    n = V7X_LANES
    nck = nb * nckb
    lane = lax.broadcasted_iota(jnp.int32, (1, n), 1)
    sgn = jnp.where(lane < n // 2, -1.0, 1.0).astype(F32)
    ucat = jnp.concatenate([u_ref[pl.ds(t, nck, stride=chunk), :] for t in range(chunk)], axis=1)
    ub = ucat.astype(BF16)
    sl = _dot(ub, bm_ref[...])
    for g in range(gt):
        sl_ref[g] = sl[:, g * n:(g + 1) * n]
    if long_scan:
        _scan_long(al_ref, init_ref, fin_ref, sl_ref, e_ref, *scan_scratch, gt=gt, nck=nck, sgn=sgn)
    else:
        _scan_short(al_ref, init_ref, fin_ref, sl_ref, e_ref, gt=gt, nb=nb, nckb=nckb, sgn=sgn)
    e = jnp.concatenate([e_ref[g] for g in range(gt)], axis=1).astype(BF16)
    y = _dot(ub, toep_ref[...]) + _dot_nt(e, cmt_ref[...])
    y = jax.nn.gelu(y + dcat_ref[...] * ucat)
    for t in range(chunk):
        y_ref[pl.ds(t, nck, stride=chunk), :] = y[:, t * n:(t + 1) * n]


def _ssm(u, tabs, layer, init, n_seq, seq_len):
    toep, bm, cmt, dcat, al = tabs
    m, ds = u.shape
    n = V7X_LANES
    chunk = SSM_CHUNK
    nj = ds // n
    gt = al.shape[1] // nj
    g_all = al.shape[1]
    assert seq_len % chunk == 0
    long_scan = (seq_len // chunk) % V7X_SUBLANES == 0
    if long_scan:
        tseg = _row_tile(seq_len, SSM_SEG_TOKENS, unit=chunk * V7X_SUBLANES)
        nseg = seq_len // tseg
        nb, nckb = 1, tseg // chunk
        grid = (nj, n_seq, nseg)
        sem = ("parallel", "parallel", "arbitrary")
        u_spec = pl.BlockSpec((tseg, n), lambda j, b, s: (b * nseg + s, j))
        st_spec = pl.BlockSpec((None, gt, n), lambda j, b, s: (b, j, 0))
        st_shape = (n_seq, g_all, n)
        tab = lambda r, c: pl.BlockSpec((None, None, r, c), lambda j, b, s: (layer, j, 0, 0))
        al_spec = pl.BlockSpec((None, gt, 2, n), lambda j, b, s: (layer, j, 0, 0))
        nck = nckb
        scan_scratch = [pltpu.VMEM((gt, nck, n), F32)] * 3 + [
            pltpu.VMEM((4 * gt, V7X_SUBLANES, n), F32), pltpu.VMEM((2 * gt, V7X_SUBLANES, n), F32)]
    else:
        nb, nckb = n_seq, seq_len // chunk
        grid = (nj,)
        sem = ("parallel",)
        u_spec = pl.BlockSpec((m, n), lambda j: (0, j))
        st_spec = pl.BlockSpec((gt, nb, n), lambda j: (j, 0, 0))
        st_shape = (g_all, n_seq, n)
        tab = lambda r, c: pl.BlockSpec((None, None, r, c), lambda j: (layer, j, 0, 0))
        al_spec = pl.BlockSpec((None, gt, 2, n), lambda j: (layer, j, 0, 0))
        nck = nb * nckb
        scan_scratch = []
    kern = functools.partial(_ssm_kernel, chunk=chunk, gt=gt, nb=nb, nckb=nckb, long_scan=long_scan)
    return pl.pallas_call(
        kern,
        grid=grid,
        in_specs=[u_spec, tab(chunk * n, chunk * n), tab(chunk * n, gt * n), tab(chunk * n, gt * n),
                  tab(1, chunk * n), al_spec, st_spec],
        out_specs=[u_spec, st_spec],
        out_shape=[jax.ShapeDtypeStruct((m, ds), F32), jax.ShapeDtypeStruct(st_shape, F32)],
        scratch_shapes=[pltpu.VMEM((gt, nck, n), F32)] * 2 + scan_scratch,
        compiler_params=_params(*sem),
        name="ssm",
    )(u, toep, bm, cmt, dcat, al, init)


def _ssm_prep_kernel(lr_ref, li_ref, ldt_ref, br_ref, bi_ref, cr_ref, ci_ref,
                     toep_ref, bm_ref, cmt_ref, al_ref, *, chunk, h):
    n = V7X_LANES
    gt = n // h
    shift = h.bit_length() - 1
    lo = lax.broadcasted_iota(jnp.int32, (1, n), 1) < n // 2
    row_group = lax.broadcasted_iota(jnp.int32, (n, 1), 0) >> shift
    col_group = lax.broadcasted_iota(jnp.int32, (1, n), 1) >> shift

    lr, li = lr_ref[...], li_ref[...]
    dt = jnp.exp(ldt_ref[...])
    mag = jnp.exp(lr * dt)
    ar, ai = mag * jnp.cos(li * dt), mag * jnp.sin(li * dt)
    den = lr * lr + li * li
    qr = ((ar - 1.0) * lr + ai * li) / den
    qi = (ai * lr - (ar - 1.0) * li) / den
    bbr, bbi = _cmul(qr, qi, br_ref[...], bi_ref[...])
    cr, ci = cr_ref[...], ci_ref[...]

    pw = [(jnp.ones_like(ar), jnp.zeros_like(ar))]
    for _ in range(chunk):
        pw.append(_cmul(*pw[-1], ar, ai))
    for g in range(gt):
        al_ref[g, 0:1, :] = pw[chunk][0][g * h:g * h + 1, :]
        al_ref[g, 1:2, :] = pw[chunk][1][g * h:g * h + 1, :]

    bb = jnp.where(lo, bbr, bbi)
    cpow = []
    for k in range(chunk + 1):
        er, ei = _cmul(cr, ci, *pw[k])
        cpow.append(jnp.where(lo, er, -ei))

    zeros = jnp.zeros((n, n), toep_ref.dtype)
    for k in range(chunk):
        kk = _dot_nt(bb, cpow[k], precision=lax.Precision.HIGHEST)
        kk = jnp.where(row_group == col_group, kk, 0.0).astype(toep_ref.dtype)
        for s in range(chunk - k):
            toep_ref[s * n:(s + 1) * n, (s + k) * n:(s + k + 1) * n] = kk
        for s in range(k, chunk):
            if k > 0:
                toep_ref[s * n:(s + 1) * n, (s - k) * n:(s - k + 1) * n] = zeros
    for s in range(chunk):
        er, ei = _cmul(*pw[chunk - 1 - s], bbr, bbi)
        e = jnp.where(lo, er, ei)
        for g in range(gt):
            bm_ref[s * n:(s + 1) * n, g * n:(g + 1) * n] = jnp.where(row_group == g, e, 0.0).astype(bm_ref.dtype)
            cmt_ref[s * n:(s + 1) * n, g * n:(g + 1) * n] = jnp.where(
                row_group == g, cpow[s + 1], 0.0).astype(cmt_ref.dtype)


def _ssm_prep(lam_re, lam_im, log_dt, b_re, b_im, c_re, c_im, d_skip):
    nl, g, p = lam_re.shape
    h = d_skip.shape[-1]
    n = V7X_LANES
    chunk = SSM_CHUNK
    assert 2 * p == n and n % h == 0 and h & (h - 1) == 0 and (g * h) % n == 0
    nj = g * h // n
    gt = n // h
    twice = lambda a: jnp.concatenate([a, a], axis=-1)
    per_row = lambda a: jnp.repeat(twice(a), h, axis=1)
    lr, li = per_row(lam_re), per_row(lam_im)
    ldt = per_row(jnp.broadcast_to(log_dt[:, :, None], (nl, g, p)))
    rows = lambda a: twice(a).reshape(nl, g * h, n)
    br, bi = rows(jnp.swapaxes(b_re, -1, -2)), rows(jnp.swapaxes(b_im, -1, -2))
    cr, ci = rows(c_re), rows(c_im)
    in_spec = pl.BlockSpec((None, n, n), lambda l, j: (l, j, 0))
    tab = lambda r, c: pl.BlockSpec((None, None, r, c), lambda l, j: (l, j, 0, 0))
    toep, bm, cmt, al = pl.pallas_call(
        functools.partial(_ssm_prep_kernel, chunk=chunk, h=h),
        grid=(nl, nj),
        in_specs=[in_spec] * 7,
        out_specs=[tab(chunk * n, chunk * n), tab(chunk * n, gt * n), tab(chunk * n, gt * n),
                   pl.BlockSpec((None, gt, 2, n), lambda l, j: (l, j, 0, 0))],
        out_shape=[
            jax.ShapeDtypeStruct((nl, nj, chunk * n, chunk * n), BF16),
            jax.ShapeDtypeStruct((nl, nj, chunk * n, gt * n), BF16),
            jax.ShapeDtypeStruct((nl, nj, chunk * n, gt * n), BF16),
            jax.ShapeDtypeStruct((nl, g, 2, n), F32),
        ],
        compiler_params=_params("parallel", "parallel"),
        name="ssm_prep",
    )(lr, li, ldt, br, bi, cr, ci)
    dcat = jnp.tile(d_skip.reshape(nl, nj, 1, n), (1, 1, 1, chunk))
    return toep, bm, cmt, dcat, al


def _mixout_kernel(x_ref, yc_ref, ys_ref, glu_ref, sg_ref, wo_ref, o_ref, *, dc):
    y = ys_ref[...]
    z = y * jax.nn.sigmoid(_dot(y.astype(BF16), glu_ref[...]))
    zn = _rmsnorm(z, sg_ref[...]).astype(BF16)
    o_ref[...] = x_ref[...] + _dot(yc_ref[...], wo_ref[0:dc, :]) + _dot(zn, wo_ref[dc:, :])


def _mixout(x, yc, ys, glu_w, ssm_g, w_out):
    m, d = x.shape
    dc, ds = yc.shape[1], ys.shape[1]
    tm = _row_tile(m, ROW_TILE)
    return pl.pallas_call(
        functools.partial(_mixout_kernel, dc=dc),
        grid=(m // tm,),
        in_specs=[
            pl.BlockSpec((tm, d), lambda i: (i, 0)),
            pl.BlockSpec((tm, dc), lambda i: (i, 0)),
            pl.BlockSpec((tm, ds), lambda i: (i, 0)),
            _resident(glu_w.shape, lambda i: (0, 0)),
            _resident((1, ds), lambda i: (0, 0)),
            _resident(w_out.shape, lambda i: (0, 0)),
        ],
        out_specs=pl.BlockSpec((tm, d), lambda i: (i, 0)),
        out_shape=jax.ShapeDtypeStruct((m, d), F32),
        compiler_params=_params("parallel"),
        name="mixout",
    )(x, yc, ys, glu_w, ssm_g.reshape(1, ds), w_out)


def _norm_kernel(x_ref, g_ref, o_ref):
    o_ref[...] = _rmsnorm(x_ref[...], g_ref[...])


def _final_norm(x, g):
    m, d = x.shape
    tm = _row_tile(m, ROW_TILE)
    return pl.pallas_call(
        _norm_kernel,
        grid=(m // tm,),
        in_specs=[pl.BlockSpec((tm, d), lambda i: (i, 0)), pl.BlockSpec((1, d), lambda i: (0, 0))],
        out_specs=pl.BlockSpec((tm, d), lambda i: (i, 0)),
        out_shape=jax.ShapeDtypeStruct((m, d), F32),
        compiler_params=_params("parallel"),
        name="final_norm",
    )(x, g.reshape(1, d))


def _trunk(x, seq_len, conv_init, ssm_init, wts, tabs):
    n_seq = x.shape[0] // seq_len
    conv_out, ssm_out = [], []
    for l in range(len(conv_init)):
        w = wts[l]
        x = _ffn(x, w["ffn1_norm"], w["ffn1_w_gate"], w["ffn1_w_up"], w["ffn1_w_down"])
        yc, u, conv_new = _inproj(x, w["mix_norm"], w["w_in"], w["conv_w"], w["conv_out_norm"],
                                  conv_init[l], seq_len)
        ys, fin = _ssm(u, tabs, l, ssm_init[l], n_seq, seq_len)
        x = _mixout(x, yc, ys, w["ssm_glu_w"], w["ssm_out_norm"], w["w_out"])
        x = _ffn(x, w["ffn2_norm"], w["ffn2_w_gate"], w["ffn2_w_up"], w["ffn2_w_down"])
        conv_out.append(conv_new)
        ssm_out.append(fin)
    return x, conv_out, ssm_out


def kernel(x_prompt, x_sample, cache_conv, state_ssm_re, state_ssm_im, meta_tokens, ffn1_norm, ffn1_w_gate, ffn1_w_up, ffn1_w_down, mix_norm, w_in, conv_w, conv_out_norm, ssm_lambda_re, ssm_lambda_im, ssm_log_dt, ssm_b_re, ssm_b_im, ssm_c_re, ssm_c_im, ssm_d, ssm_glu_w, ssm_out_norm, w_out, ffn2_norm, ffn2_w_gate, ffn2_w_up, ffn2_w_down, final_norm):
    bsz, seq, d = x_prompt.shape
    dec_b, dec_t, _ = x_sample.shape
    n_meta = meta_tokens.shape[0]
    depth, g, p = ssm_lambda_re.shape
    dc = conv_w.shape[-1]

    tabs = _ssm_prep(ssm_lambda_re, ssm_lambda_im, ssm_log_dt, ssm_b_re, ssm_b_im,
                     ssm_c_re, ssm_c_im, ssm_d)
    cast = lambda a: a.astype(BF16)
    wts = [dict(
        ffn1_norm=ffn1_norm[l], ffn1_w_gate=cast(ffn1_w_gate[l]), ffn1_w_up=cast(ffn1_w_up[l]),
        ffn1_w_down=cast(ffn1_w_down[l]), mix_norm=mix_norm[l], w_in=cast(w_in[l]),
        conv_w=conv_w[l], conv_out_norm=conv_out_norm[l], ssm_glu_w=cast(ssm_glu_w[l]),
        ssm_out_norm=ssm_out_norm[l], w_out=cast(w_out[l]), ffn2_norm=ffn2_norm[l],
        ffn2_w_gate=cast(ffn2_w_gate[l]), ffn2_w_up=cast(ffn2_w_up[l]),
        ffn2_w_down=cast(ffn2_w_down[l])) for l in range(depth)]

    meta_b = V7X_SUBLANES
    xm = jnp.tile(meta_tokens.astype(F32), (meta_b, 1))
    zc = [jnp.zeros((meta_b, 2, dc), F32)] * depth
    zs = [jnp.zeros((g, meta_b, 2 * p), F32)] * depth
    _, conv_m, ssm_m = _trunk(xm, n_meta, zc, zs, wts, tabs)

    s_init = jnp.transpose(jnp.concatenate([state_ssm_re, state_ssm_im], axis=-1), (0, 2, 1, 3))
    xs, conv_s, ssm_s = _trunk(x_sample.reshape(dec_b * dec_t, d), dec_t,
                               [cache_conv[l] for l in range(depth)],
                               [s_init[l] for l in range(depth)], wts, tabs)

    conv_p0 = [jnp.broadcast_to(c[0:1], (bsz, 2, dc)) for c in conv_m]
    ssm_p0 = [jnp.broadcast_to(s[None, :, 0], (bsz, g, 2 * p)) for s in ssm_m]
    xp, conv_p, ssm_p = _trunk(x_prompt.reshape(bsz * seq, d), seq, conv_p0, ssm_p0, wts, tabs)

    y_prompt = _final_norm(xp, final_norm).reshape(bsz, seq, d)
    y_sample = _final_norm(xs, final_norm).reshape(dec_b, dec_t, d)

    sp = jnp.stack(ssm_p)
    ss = jnp.transpose(jnp.stack(ssm_s), (0, 2, 1, 3))
    return (y_prompt, y_sample, jnp.stack(conv_p), sp[..., :p], sp[..., p:],
            jnp.stack(conv_s), ss[..., :p], ss[..., p:])
```

```python
import functools

import jax
import jax.numpy as jnp
from jax import lax
from jax.experimental import pallas as pl
from jax.experimental.pallas import tpu as pltpu

F32 = jnp.float32
BF16 = jnp.bfloat16
EPS = 1e-6

V7X_LANES = 128
V7X_SUBLANES = 8
V7X_VMEM_LIMIT_BYTES = 56 * 1024 * 1024
V7X_VMEM_LIMIT_BYTES_FFN = 60 * 1024 * 1024

ROW_TILE = 512
FFN_ROW_TILE = 1024
FF_TILE = 512
FFN_EDGE_ROWS = 128
SSM_CHUNK = V7X_SUBLANES
SSM_SEG_TOKENS = 4096


def _row_tile(m, target, unit=V7X_SUBLANES):
    if m <= target:
        return m
    t = (target // unit) * unit
    while t > unit and m % t:
        t -= unit
    assert m % t == 0, (m, target)
    return t


def _lane_tile(n, target):
    t = (min(n, target) // V7X_LANES) * V7X_LANES
    while t > V7X_LANES and n % t:
        t -= V7X_LANES
    assert t > 0 and n % t == 0, (n, target)
    return t


def _params(*sem, vmem=V7X_VMEM_LIMIT_BYTES):
    return pltpu.CompilerParams(dimension_semantics=sem, vmem_limit_bytes=vmem)


def _resident(block_shape, index_map):
    return pl.BlockSpec(block_shape, index_map, pipeline_mode=pl.Buffered(1))


def _rmsnorm(x, g):
    ms = jnp.mean(x * x, axis=-1, keepdims=True)
    return x * lax.rsqrt(ms + EPS) * g


def _dot(a, b):
    return jnp.dot(a, b, preferred_element_type=F32)


def _dot_nt(a, b, precision=None):
    return lax.dot_general(a, b, (((1,), (1,)), ((), ())), preferred_element_type=F32,
                           precision=precision)


def _ffn_kernel(x_ref, g_ref, wg_ref, wu_ref, wd_ref, *rest, final_norm):
    if final_norm:
        gf_ref, o_ref, xn_ref = rest
    else:
        o_ref, xn_ref = rest
    j = pl.program_id(1)
    tm = x_ref.shape[0]
    sub_rows = [slice(r, r + min(tm, FFN_EDGE_ROWS)) for r in range(0, tm, min(tm, FFN_EDGE_ROWS))]

    @pl.when(j == 0)
    def _():
        for rows in sub_rows:
            xn_ref[rows, :] = _rmsnorm(x_ref[rows, :], g_ref[...]).astype(BF16)
            o_ref[rows, :] = jnp.zeros((rows.stop - rows.start, o_ref.shape[1]), F32)

    xn = xn_ref[...]
    hg = _dot(xn, wg_ref[...])
    hu = _dot(xn, wu_ref[...])
    h = (jax.nn.silu(hg) * hu).astype(BF16)
    o_ref[...] += _dot(h, wd_ref[...])

    @pl.when(j == pl.num_programs(1) - 1)
    def _():
        for rows in sub_rows:
            y = x_ref[rows, :] + 0.5 * o_ref[rows, :]
            o_ref[rows, :] = _rmsnorm(y, gf_ref[...]) if final_norm else y


def _ffn(x, layer, g, wg, wu, wd, final_gain=None):
    m, d = x.shape
    f = wg.shape[-1]
    tm = _row_tile(m, FFN_ROW_TILE if final_gain is None else FFN_ROW_TILE // 2)
    tf = _lane_tile(f, FF_TILE)
    in_specs = [
        pl.BlockSpec((tm, d), lambda i, j: (i, 0)),
        pl.BlockSpec((None, 1, d), lambda i, j: (layer, 0, 0)),
        pl.BlockSpec((None, d, tf), lambda i, j: (layer, 0, j)),
        pl.BlockSpec((None, d, tf), lambda i, j: (layer, 0, j)),
        pl.BlockSpec((None, tf, d), lambda i, j: (layer, j, 0)),
    ]
    args = [x, g, wg, wu, wd]
    if final_gain is not None:
        in_specs.append(pl.BlockSpec((1, d), lambda i, j: (0, 0)))
        args.append(final_gain)
    return pl.pallas_call(
        functools.partial(_ffn_kernel, final_norm=final_gain is not None),
        grid=(m // tm, f // tf),
        in_specs=in_specs,
        out_specs=pl.BlockSpec((tm, d), lambda i, j: (i, 0)),
        out_shape=jax.ShapeDtypeStruct((m, d), F32),
        scratch_shapes=[pltpu.VMEM((tm, d), BF16)],
        compiler_params=_params("parallel", "arbitrary", vmem=V7X_VMEM_LIMIT_BYTES_FFN),
        name="ffn",
    )(*args)


def _inproj_kernel(x_ref, g_ref, w_ref, cw_ref, cg_ref, cinit_ref,
                   yc_ref, u_ref, cnew_ref, carry_ref, *, ns, ts, blocks_per_seq, dc):
    i = pl.program_id(0)
    xn = _rmsnorm(x_ref[...], g_ref[...]).astype(BF16)
    b_gate = _dot(xn, w_ref[:, 0:dc])
    v = _dot(xn, w_ref[:, dc:2 * dc]) * _dot(xn, w_ref[:, 2 * dc:3 * dc])
    u_ref[...] = _dot(xn, w_ref[:, 3 * dc:])

    if blocks_per_seq > 1:
        @pl.when((i % blocks_per_seq) == 0)
        def _():
            carry_ref[0:2, :] = cinit_ref[0]

    row = lax.broadcasted_iota(jnp.int32, (ts, 1), 0)
    w0, w1, w2 = cw_ref[0:1, :], cw_ref[1:2, :], cw_ref[2:3, :]
    for s in range(ns):
        vs = v[s * ts:(s + 1) * ts]
        prev = carry_ref if blocks_per_seq > 1 else cinit_ref.at[s]
        p0, p1 = prev[0:1, :], prev[1:2, :]
        vm1 = jnp.where(row == 0, p1, pltpu.roll(vs, 1, 0))
        vm2 = jnp.where(row == 0, p0, jnp.where(row == 1, p1, pltpu.roll(vs, 2, 0)))
        y = b_gate[s * ts:(s + 1) * ts] * (w0 * vm2 + w1 * vm1 + w2 * vs)
        yc_ref[s * ts:(s + 1) * ts, :] = _rmsnorm(y, cg_ref[...]).astype(yc_ref.dtype)
        tail = vs[ts - 2:ts]
        cnew_ref[s] = tail
        if blocks_per_seq > 1:
            carry_ref[0:2, :] = tail


def _inproj(x, layer, g, w_in, conv_w, conv_g, conv_init, seq_len):
    m, d = x.shape
    n_seq = m // seq_len
    dc = conv_w.shape[-1]
    ds = w_in.shape[-1] - 3 * dc
    if seq_len >= ROW_TILE:
        tm = _row_tile(seq_len, ROW_TILE)
        ns, ts, bps = 1, tm, seq_len // tm
    else:
        tm, ns, ts, bps = m, n_seq, seq_len, 1
    kern = functools.partial(_inproj_kernel, ns=ns, ts=ts, blocks_per_seq=bps, dc=dc)
    per_layer = lambda a: _resident((None,) + a.shape[1:], lambda i: (layer, 0, 0))
    return pl.pallas_call(
        kern,
        grid=(m // tm,),
        in_specs=[
            pl.BlockSpec((tm, d), lambda i: (i, 0)),
            per_layer(g), per_layer(w_in), per_layer(conv_w), per_layer(conv_g),
            pl.BlockSpec((ns, 2, dc), lambda i: (i // bps, 0, 0)),
        ],
        out_specs=[
            pl.BlockSpec((tm, dc), lambda i: (i, 0)),
            pl.BlockSpec((tm, ds), lambda i: (i, 0)),
            pl.BlockSpec((ns, 2, dc), lambda i: (i // bps, 0, 0)),
        ],
        out_shape=[
            jax.ShapeDtypeStruct((m, dc), BF16),
            jax.ShapeDtypeStruct((m, ds), F32),
            jax.ShapeDtypeStruct((n_seq, 2, dc), F32),
        ],
        scratch_shapes=[pltpu.VMEM((V7X_SUBLANES, dc), F32)],
        compiler_params=_params("arbitrary"),
        name="inproj",
    )(x, g, w_in, conv_w, conv_g, conv_init)


def _cmul(ar, ai, br, bi):
    return ar * br - ai * bi, ar * bi + ai * br


def _scan_short(al_ref, init_ref, fin_ref, sl_ref, e_ref, *, npair, nb, nckb):
    for q in range(npair):
        a_r, a_i = al_ref[q, 0:1, :], al_ref[q, 1:2, :]
        e_r, e_i = init_ref[q, 0], init_ref[q, 1]
        for c in range(nckb):
            rows = pl.ds(c, nb, stride=nckb)
            e_ref[2 * q, rows, :] = e_r
            e_ref[2 * q + 1, rows, :] = e_i
            e_r, e_i = _cmul(a_r, a_i, e_r, e_i)
            e_r, e_i = e_r + sl_ref[2 * q, rows, :], e_i + sl_ref[2 * q + 1, rows, :]
        fin_ref[q, 0] = e_r
        fin_ref[q, 1] = e_i


def _scan_long(al_ref, init_ref, fin_ref, sl_ref, e_ref, v_ref, st_ref, *, npair, nck):
    n, sub = V7X_LANES, V7X_SUBLANES
    nv = nck // sub
    seg = pl.program_id(2)
    row8 = lax.broadcasted_iota(jnp.int32, (sub, n), 0)

    @pl.when(seg == 0)
    def _():
        for q in range(npair):
            st_ref[2 * q] = jnp.broadcast_to(init_ref[q, 0:1, :], (sub, n))
            st_ref[2 * q + 1] = jnp.broadcast_to(init_ref[q, 1:2, :], (sub, n))

    tabs = []
    for q in range(npair):
        a_r, a_i = al_ref[q, 0:1, :], al_ref[q, 1:2, :]
        pw = [(jnp.ones_like(a_r), jnp.zeros_like(a_r)), (a_r, a_i)]
        for k in range(2, sub + 1):
            pw.append(_cmul(*pw[k // 2], *pw[k - k // 2]))
        v_r = sl_ref[2 * q].reshape(nv, sub, n)
        v_i = sl_ref[2 * q + 1].reshape(nv, sub, n)
        for k in (1, 2, 4):
            c_r = jnp.where(row8 >= k, pw[k][0], 0.0)[None]
            c_i = jnp.where(row8 >= k, pw[k][1], 0.0)[None]
            s_r, s_i = pltpu.roll(v_r, k, 1), pltpu.roll(v_i, k, 1)
            v_r, v_i = v_r + (c_r * s_r - c_i * s_i), v_i + (c_r * s_i + c_i * s_r)
        v_ref[2 * q] = v_r.reshape(nck, n)
        v_ref[2 * q + 1] = v_i.reshape(nck, n)
        keep = jnp.where(row8 >= 1, 1.0, 0.0)[None]
        e_ref[2 * q] = (pltpu.roll(v_r, 1, 1) * keep).reshape(nck, n)
        e_ref[2 * q + 1] = (pltpu.roll(v_i, 1, 1) * keep).reshape(nck, n)
        ent_r = jnp.zeros((sub, n), F32)
        ent_i = jnp.zeros((sub, n), F32)
        for r in range(sub):
            ent_r = jnp.where(row8 == r, pw[r][0], ent_r)
            ent_i = jnp.where(row8 == r, pw[r][1], ent_i)
        tabs.append((ent_r, ent_i, jnp.broadcast_to(pw[sub][0], (sub, n)),
                     jnp.broadcast_to(pw[sub][1], (sub, n))))

    state = [st_ref[i] for i in range(2 * npair)]
    for k in range(nv):
        rows = slice(k * sub, (k + 1) * sub)
        for q in range(npair):
            ent_r, ent_i, step_r, step_i = tabs[q]
            e0_r, e0_i = state[2 * q], state[2 * q + 1]
            c_r, c_i = _cmul(ent_r, ent_i, e0_r, e0_i)
            e_ref[2 * q, rows, :] += c_r
            e_ref[2 * q + 1, rows, :] += c_i
            last_r = jnp.broadcast_to(v_ref[2 * q, (k + 1) * sub - 1:(k + 1) * sub, :], (sub, n))
            last_i = jnp.broadcast_to(v_ref[2 * q + 1, (k + 1) * sub - 1:(k + 1) * sub, :], (sub, n))
            n_r, n_i = _cmul(step_r, step_i, e0_r, e0_i)
            state[2 * q], state[2 * q + 1] = n_r + last_r, n_i + last_i
    for i in range(2 * npair):
        st_ref[i] = state[i]

    @pl.when(seg == pl.num_programs(2) - 1)
    def _():
        for q in range(npair):
            fin_ref[q, 0:1, :] = state[2 * q][0:1, :]
            fin_ref[q, 1:2, :] = state[2 * q + 1][0:1, :]


def _ssm_kernel(u_ref, toep_ref, bm_ref, cmt_ref, dcat_ref, al_ref, init_ref, y_ref, fin_ref,
                sl_ref, e_ref, *scan_scratch, chunk, npair, nb, nckb, long_scan):
    n = V7X_LANES
    nck = nb * nckb
    ucat = jnp.concatenate([u_ref[pl.ds(t, nck, stride=chunk), :] for t in range(chunk)], axis=1)
    ub = ucat.astype(BF16)
    sl = _dot(ub, bm_ref[...])
    for g in range(2 * npair):
        sl_ref[g] = sl[:, g * n:(g + 1) * n]
    if long_scan:
        _scan_long(al_ref, init_ref, fin_ref, sl_ref, e_ref, *scan_scratch, npair=npair, nck=nck)
    else:
        _scan_short(al_ref, init_ref, fin_ref, sl_ref, e_ref, npair=npair, nb=nb, nckb=nckb)
    e = jnp.concatenate([e_ref[g] for g in range(2 * npair)], axis=1).astype(BF16)
    y = _dot(ub, toep_ref[...]) + _dot_nt(e, cmt_ref[...])
    y = jax.nn.gelu(y + dcat_ref[...] * ucat)
    for t in range(chunk):
        y_ref[pl.ds(t, nck, stride=chunk), :] = y[:, t * n:(t + 1) * n]


def _ssm(u, tabs, layer, init, n_seq, seq_len):
    toep, bm, cmt, dcat, al = tabs
    m, ds = u.shape
    n = V7X_LANES
    chunk = SSM_CHUNK
    nj = ds // n
    npair_all = al.shape[1]
    npair = npair_all // nj
    assert seq_len % chunk == 0
    long_scan = (seq_len // chunk) % V7X_SUBLANES == 0
    if long_scan:
        tseg = _row_tile(seq_len, SSM_SEG_TOKENS, unit=chunk * V7X_SUBLANES)
        nseg = seq_len // tseg
        nb, nckb = 1, tseg // chunk
        grid = (nj, n_seq, nseg)
        sem = ("parallel", "parallel", "arbitrary")
        u_spec = pl.BlockSpec((tseg, n), lambda j, b, s: (b * nseg + s, j))
        st_spec = pl.BlockSpec((None, npair, 2, n), lambda j, b, s: (b, j, 0, 0))
        st_shape = (n_seq, npair_all, 2, n)
        tab = lambda r, c: pl.BlockSpec((None, None, r, c), lambda j, b, s: (layer, j, 0, 0))
        al_spec = pl.BlockSpec((None, npair, 2, n), lambda j, b, s: (layer, j, 0, 0))
        nck = nckb
        scan_scratch = [pltpu.VMEM((2 * npair, nck, n), F32),
                        pltpu.VMEM((2 * npair, V7X_SUBLANES, n), F32)]
    else:
        nb, nckb = n_seq, seq_len // chunk
        grid = (nj,)
        sem = ("parallel",)
        u_spec = pl.BlockSpec((m, n), lambda j: (0, j))
        st_spec = pl.BlockSpec((npair, 2, nb, n), lambda j: (j, 0, 0, 0))
        st_shape = (npair_all, 2, n_seq, n)
        tab = lambda r, c: pl.BlockSpec((None, None, r, c), lambda j: (layer, j, 0, 0))
        al_spec = pl.BlockSpec((None, npair, 2, n), lambda j: (layer, j, 0, 0))
        nck = nb * nckb
        scan_scratch = []
    kern = functools.partial(_ssm_kernel, chunk=chunk, npair=npair, nb=nb, nckb=nckb,
                             long_scan=long_scan)
    return pl.pallas_call(
        kern,
        grid=grid,
        in_specs=[u_spec, tab(chunk * n, chunk * n), tab(chunk * n, 2 * npair * n),
                  tab(chunk * n, 2 * npair * n), tab(1, chunk * n), al_spec, st_spec],
        out_specs=[u_spec, st_spec],
        out_shape=[jax.ShapeDtypeStruct((m, ds), F32), jax.ShapeDtypeStruct(st_shape, F32)],
        scratch_shapes=[pltpu.VMEM((2 * npair, nck, n), F32)] * 2 + scan_scratch,
        compiler_params=_params(*sem),
        name="ssm",
    )(u, toep, bm, cmt, dcat, al, init)


def _ssm_prep_kernel(lr_ref, li_ref, ldt_ref, br_ref, bi_ref, cr_ref, ci_ref,
                     toep_ref, bm_ref, cmt_ref, al_ref, *, chunk, h):
    n = V7X_LANES
    npair = n // h // 2
    shift = h.bit_length() - 1
    lo = lax.broadcasted_iota(jnp.int32, (1, n), 1) < n // 2
    row_group = lax.broadcasted_iota(jnp.int32, (n, 1), 0) >> shift
    col_group = lax.broadcasted_iota(jnp.int32, (1, n), 1) >> shift
    pair_mask = [((row_group == 2 * q) & lo) | ((row_group == 2 * q + 1) & ~lo) for q in range(npair)]

    lr, li = lr_ref[...], li_ref[...]
    dt = jnp.exp(ldt_ref[...])
    mag = jnp.exp(lr * dt)
    ar, ai = mag * jnp.cos(li * dt), mag * jnp.sin(li * dt)
    den = lr * lr + li * li
    qr = ((ar - 1.0) * lr + ai * li) / den
    qi = (ai * lr - (ar - 1.0) * li) / den
    bbr, bbi = _cmul(qr, qi, br_ref[...], bi_ref[...])
    cr, ci = cr_ref[...], ci_ref[...]

    pw = [(jnp.ones_like(ar), jnp.zeros_like(ar))]
    for _ in range(chunk):
        pw.append(_cmul(*pw[-1], ar, ai))
    for q in range(npair):
        even, odd = 2 * q * h, (2 * q + 1) * h
        for part in range(2):
            al_ref[q, part:part + 1, :] = jnp.where(lo, pw[chunk][part][even:even + 1, :],
                                                    pw[chunk][part][odd:odd + 1, :])

    cpow = [_cmul(cr, ci, *pw[k]) for k in range(chunk + 1)]

    bb = jnp.where(lo, bbr, bbi)
    zeros = jnp.zeros((n, n), toep_ref.dtype)
    for k in range(chunk):
        kk = _dot_nt(bb, jnp.where(lo, cpow[k][0], -cpow[k][1]), precision=lax.Precision.HIGHEST)
        kk = jnp.where(row_group == col_group, kk, 0.0).astype(toep_ref.dtype)
        for s in range(chunk - k):
            toep_ref[s * n:(s + 1) * n, (s + k) * n:(s + k + 1) * n] = kk
        for s in range(k, chunk):
            if k > 0:
                toep_ref[s * n:(s + 1) * n, (s - k) * n:(s - k + 1) * n] = zeros
    for s in range(chunk):
        rows = slice(s * n, (s + 1) * n)
        er, ei = _cmul(*pw[chunk - 1 - s], bbr, bbi)
        for q in range(npair):
            c_re, c_im = slice(2 * q * n, (2 * q + 1) * n), slice((2 * q + 1) * n, (2 * q + 2) * n)
            bm_ref[rows, c_re] = jnp.where(pair_mask[q], er, 0.0).astype(bm_ref.dtype)
            bm_ref[rows, c_im] = jnp.where(pair_mask[q], ei, 0.0).astype(bm_ref.dtype)
            cmt_ref[rows, c_re] = jnp.where(pair_mask[q], cpow[s + 1][0], 0.0).astype(cmt_ref.dtype)
            cmt_ref[rows, c_im] = jnp.where(pair_mask[q], -cpow[s + 1][1], 0.0).astype(cmt_ref.dtype)


def _ssm_prep(lam_re, lam_im, log_dt, b_re, b_im, c_re, c_im, d_skip):
    nl, g, p = lam_re.shape
    h = d_skip.shape[-1]
    n = V7X_LANES
    chunk = SSM_CHUNK
    assert 2 * p == n and n % (2 * h) == 0 and h & (h - 1) == 0 and (g * h) % n == 0
    nj = g * h // n
    npair = n // h // 2
    twice = lambda a: jnp.concatenate([a, a], axis=-1)
    per_row = lambda a: jnp.repeat(twice(a), h, axis=1)
    lr, li = per_row(lam_re), per_row(lam_im)
    ldt = per_row(jnp.broadcast_to(log_dt[:, :, None], (nl, g, p)))
    rows = lambda a: twice(a).reshape(nl, g * h, n)
    br, bi = rows(jnp.swapaxes(b_re, -1, -2)), rows(jnp.swapaxes(b_im, -1, -2))
    cr, ci = rows(c_re), rows(c_im)
    in_spec = pl.BlockSpec((None, n, n), lambda l, j: (l, j, 0))
    tab = lambda r, c: pl.BlockSpec((None, None, r, c), lambda l, j: (l, j, 0, 0))
    toep, bm, cmt, al = pl.pallas_call(
        functools.partial(_ssm_prep_kernel, chunk=chunk, h=h),
        grid=(nl, nj),
        in_specs=[in_spec] * 7,
        out_specs=[tab(chunk * n, chunk * n), tab(chunk * n, 2 * npair * n),
                   tab(chunk * n, 2 * npair * n),
                   pl.BlockSpec((None, npair, 2, n), lambda l, j: (l, j, 0, 0))],
        out_shape=[
            jax.ShapeDtypeStruct((nl, nj, chunk * n, chunk * n), BF16),
            jax.ShapeDtypeStruct((nl, nj, chunk * n, 2 * npair * n), BF16),
            jax.ShapeDtypeStruct((nl, nj, chunk * n, 2 * npair * n), BF16),
            jax.ShapeDtypeStruct((nl, g // 2, 2, n), F32),
        ],
        compiler_params=_params("parallel", "parallel"),
        name="ssm_prep",
    )(lr, li, ldt, br, bi, cr, ci)
    dcat = jnp.tile(d_skip.reshape(nl, nj, 1, n), (1, 1, 1, chunk))
    return toep, bm, cmt, dcat, al


def _mixout_kernel(x_ref, yc_ref, ys_ref, glu_ref, sg_ref, wo_ref, o_ref, *, dc):
    y = ys_ref[...]
    z = y * jax.nn.sigmoid(_dot(y.astype(BF16), glu_ref[...]))
    zn = _rmsnorm(z, sg_ref[...]).astype(BF16)
    o_ref[...] = x_ref[...] + _dot(yc_ref[...], wo_ref[0:dc, :]) + _dot(zn, wo_ref[dc:, :])


def _mixout(x, yc, ys, layer, glu_w, ssm_g, w_out):
    m, d = x.shape
    dc, ds = yc.shape[1], ys.shape[1]
    tm = _row_tile(m, ROW_TILE)
    per_layer = lambda a: _resident((None,) + a.shape[1:], lambda i: (layer, 0, 0))
    return pl.pallas_call(
        functools.partial(_mixout_kernel, dc=dc),
        grid=(m // tm,),
        in_specs=[
            pl.BlockSpec((tm, d), lambda i: (i, 0)),
            pl.BlockSpec((tm, dc), lambda i: (i, 0)),
            pl.BlockSpec((tm, ds), lambda i: (i, 0)),
            per_layer(glu_w), per_layer(ssm_g), per_layer(w_out),
        ],
        out_specs=pl.BlockSpec((tm, d), lambda i: (i, 0)),
        out_shape=jax.ShapeDtypeStruct((m, d), F32),
        compiler_params=_params("parallel"),
        name="mixout",
    )(x, yc, ys, glu_w, ssm_g, w_out)


def _trunk(x, seq_len, conv_init, ssm_init, w, tabs, final_gain):
    n_seq = x.shape[0] // seq_len
    depth = len(conv_init)
    conv_out, ssm_out = [], []
    for l in range(depth):
        x = _ffn(x, l, w["ffn1_norm"], w["ffn1_w_gate"], w["ffn1_w_up"], w["ffn1_w_down"])
        yc, u, conv_new = _inproj(x, l, w["mix_norm"], w["w_in"], w["conv_w"], w["conv_out_norm"],
                                  conv_init[l], seq_len)
        ys, fin = _ssm(u, tabs, l, ssm_init[l], n_seq, seq_len)
        x = _mixout(x, yc, ys, l, w["ssm_glu_w"], w["ssm_out_norm"], w["w_out"])
        x = _ffn(x, l, w["ffn2_norm"], w["ffn2_w_gate"], w["ffn2_w_up"], w["ffn2_w_down"],
                 final_gain=final_gain if l == depth - 1 else None)
        conv_out.append(conv_new)
        ssm_out.append(fin)
    return x, conv_out, ssm_out


def kernel(x_prompt, x_sample, cache_conv, state_ssm_re, state_ssm_im, meta_tokens, ffn1_norm, ffn1_w_gate, ffn1_w_up, ffn1_w_down, mix_norm, w_in, conv_w, conv_out_norm, ssm_lambda_re, ssm_lambda_im, ssm_log_dt, ssm_b_re, ssm_b_im, ssm_c_re, ssm_c_im, ssm_d, ssm_glu_w, ssm_out_norm, w_out, ffn2_norm, ffn2_w_gate, ffn2_w_up, ffn2_w_down, final_norm):
    bsz, seq, d = x_prompt.shape
    dec_b, dec_t, _ = x_sample.shape
    n_meta = meta_tokens.shape[0]
    depth, g, p = ssm_lambda_re.shape
    dc = conv_w.shape[-1]
    n = V7X_LANES
    assert (seq // SSM_CHUNK) % V7X_SUBLANES == 0 and (dec_t // SSM_CHUNK) % V7X_SUBLANES != 0

    tabs = _ssm_prep(ssm_lambda_re, ssm_lambda_im, ssm_log_dt, ssm_b_re, ssm_b_im,
                     ssm_c_re, ssm_c_im, ssm_d)
    cast = lambda a: a.astype(BF16)
    gain = lambda a: a[:, None, :]
    w = dict(
        ffn1_norm=gain(ffn1_norm), ffn1_w_gate=cast(ffn1_w_gate), ffn1_w_up=cast(ffn1_w_up),
        ffn1_w_down=cast(ffn1_w_down), mix_norm=gain(mix_norm), w_in=cast(w_in), conv_w=conv_w,
        conv_out_norm=gain(conv_out_norm), ssm_glu_w=cast(ssm_glu_w),
        ssm_out_norm=gain(ssm_out_norm), w_out=cast(w_out), ffn2_norm=gain(ffn2_norm),
        ffn2_w_gate=cast(ffn2_w_gate), ffn2_w_up=cast(ffn2_w_up), ffn2_w_down=cast(ffn2_w_down))
    final_gain = final_norm.reshape(1, d)

    meta_b = V7X_SUBLANES
    xm = jnp.tile(meta_tokens.astype(F32), (meta_b, 1))
    zc = [jnp.zeros((meta_b, 2, dc), F32)] * depth
    zs = [jnp.zeros((g // 2, 2, meta_b, n), F32)] * depth
    _, conv_m, ssm_m = _trunk(xm, n_meta, zc, zs, w, tabs, final_gain)

    pairs = lambda a: a.reshape(depth, -1, g // 2, n)
    s_init = jnp.transpose(jnp.stack([pairs(state_ssm_re), pairs(state_ssm_im)], axis=3),
                           (0, 2, 3, 1, 4))
    xs, conv_s, ssm_s = _trunk(x_sample.reshape(dec_b * dec_t, d), dec_t,
                               [cache_conv[l] for l in range(depth)],
                               [s_init[l] for l in range(depth)], w, tabs, final_gain)

    conv_p0 = [jnp.broadcast_to(c[0:1], (bsz, 2, dc)) for c in conv_m]
    ssm_p0 = [jnp.broadcast_to(s[None, :, :, 0], (bsz, g // 2, 2, n)) for s in ssm_m]
    xp, conv_p, ssm_p = _trunk(x_prompt.reshape(bsz * seq, d), seq, conv_p0, ssm_p0, w, tabs,
                               final_gain)

    sp = jnp.stack(ssm_p)
    ss = jnp.transpose(jnp.stack(ssm_s), (0, 3, 1, 2, 4))
    part = lambda s, i: s[:, :, :, i, :].reshape(depth, -1, g, p)
    return (xp.reshape(bsz, seq, d), xs.reshape(dec_b, dec_t, d), jnp.stack(conv_p),
            part(sp, 0), part(sp, 1), jnp.stack(conv_s), part(ss, 0), part(ss, 1))
```

```python
import functools

import jax
import jax.numpy as jnp
from jax import lax
from jax.experimental import pallas as pl
from jax.experimental.pallas import tpu as pltpu

F32 = jnp.float32
BF16 = jnp.bfloat16
EPS = 1e-6

V7X_LANES = 128
V7X_SUBLANES = 8
V7X_VMEM_LIMIT_BYTES = 56 * 1024 * 1024
V7X_VMEM_LIMIT_BYTES_BIG_TILES = 60 * 1024 * 1024

ROW_TILE = 512
INPROJ_ROW_TILE = 1024
FFN_ROW_TILE = 1024
FF_TILE = 512
FFN_EDGE_ROWS = 128
SSM_CHUNK = V7X_SUBLANES
SSM_SEG_TOKENS = 8192
SSM_ROW_PARTS = 2


def _row_tile(m, target, unit=V7X_SUBLANES):
    if m <= target:
        return m
    t = (target // unit) * unit
    while t > unit and m % t:
        t -= unit
    assert m % t == 0, (m, target)
    return t


def _lane_tile(n, target):
    t = (min(n, target) // V7X_LANES) * V7X_LANES
    while t > V7X_LANES and n % t:
        t -= V7X_LANES
    assert t > 0 and n % t == 0, (n, target)
    return t


def _params(*sem, vmem=V7X_VMEM_LIMIT_BYTES):
    return pltpu.CompilerParams(dimension_semantics=sem, vmem_limit_bytes=vmem)


def _resident(block_shape, index_map):
    return pl.BlockSpec(block_shape, index_map, pipeline_mode=pl.Buffered(1))


def _rmsnorm(x, g):
    ms = jnp.mean(x * x, axis=-1, keepdims=True)
    return x * lax.rsqrt(ms + EPS) * g


def _dot(a, b):
    return jnp.dot(a, b, preferred_element_type=F32)


def _dot_nt(a, b, precision=None):
    return lax.dot_general(a, b, (((1,), (1,)), ((), ())), preferred_element_type=F32,
                           precision=precision)


def _ffn_kernel(x_ref, g_ref, wg_ref, wu_ref, wd_ref, *rest, final_norm, emit_bf16):
    rest = list(rest)
    gf_ref = rest.pop(0) if final_norm else None
    o_ref = rest.pop(0)
    casts = [rest.pop(0) for _ in range(3)] if emit_bf16 else None
    xn_ref, = rest
    j = pl.program_id(1)
    tm = x_ref.shape[0]
    edge = _row_tile(tm, FFN_EDGE_ROWS, unit=2 * V7X_SUBLANES)

    def for_sub_blocks(fn, rolled=False):
        if rolled:
            def body(r, carry):
                fn(pl.ds(pl.multiple_of(r * edge, edge), edge))
                return carry
            lax.fori_loop(0, tm // edge, body, 0)
        else:
            for r in range(0, tm, edge):
                fn(pl.ds(r, edge))

    def step(first):
        xn = xn_ref[...]
        wg, wu, wd = (w[...].astype(BF16) for w in (wg_ref, wu_ref, wd_ref))
        if emit_bf16:
            for dst, w in zip(casts, (wg, wu, wd)):
                dst[...] = w
        h = (jax.nn.silu(_dot(xn, wg)) * _dot(xn, wu)).astype(BF16)
        if first:
            o_ref[...] = _dot(h, wd)
        else:
            o_ref[...] += _dot(h, wd)

    @pl.when(j == 0)
    def _():
        def norm_in(rows):
            xn_ref[rows, :] = _rmsnorm(x_ref[rows, :], g_ref[...]).astype(BF16)
        for_sub_blocks(norm_in)
        step(True)

    @pl.when(j > 0)
    def _():
        step(False)

    @pl.when(j == pl.num_programs(1) - 1)
    def _():
        def residual(rows):
            y = x_ref[rows, :] + 0.5 * o_ref[rows, :]
            o_ref[rows, :] = _rmsnorm(y, gf_ref[...]) if final_norm else y
        for_sub_blocks(residual, rolled=final_norm)


def _ffn(x, g, layer, wg, wu, wd, w_layer, final_gain=None, emit_bf16=False):
    m, d = x.shape
    f = wg.shape[-1]
    tm = _row_tile(m, FFN_ROW_TILE)
    tf = _lane_tile(f, FF_TILE)
    in_specs = [
        pl.BlockSpec((tm, d), lambda i, j: (i, 0)),
        pl.BlockSpec((None, 1, d), lambda i, j: (layer, 0, 0)),
        pl.BlockSpec((None, d, tf), lambda i, j: (w_layer, 0, j)),
        pl.BlockSpec((None, d, tf), lambda i, j: (w_layer, 0, j)),
        pl.BlockSpec((None, tf, d), lambda i, j: (w_layer, j, 0)),
    ]
    args = [x, g, wg, wu, wd]
    if final_gain is not None:
        in_specs.append(pl.BlockSpec((1, d), lambda i, j: (0, 0)))
        args.append(final_gain)
    out_specs = [pl.BlockSpec((tm, d), lambda i, j: (i, 0))]
    out_shape = [jax.ShapeDtypeStruct((m, d), F32)]
    if emit_bf16:
        assert m == tm, "each weight tile must be visited once"
        out_specs += [pl.BlockSpec((None, d, tf), lambda i, j: (0, 0, j))] * 2
        out_specs += [pl.BlockSpec((None, tf, d), lambda i, j: (0, j, 0))]
        out_shape += [jax.ShapeDtypeStruct((1, d, f), BF16)] * 2 + [jax.ShapeDtypeStruct((1, f, d), BF16)]
    out = pl.pallas_call(
        functools.partial(_ffn_kernel, final_norm=final_gain is not None, emit_bf16=emit_bf16),
        grid=(m // tm, f // tf),
        in_specs=in_specs,
        out_specs=out_specs,
        out_shape=out_shape,
        scratch_shapes=[pltpu.VMEM((tm, d), BF16)],
        compiler_params=_params("parallel", "arbitrary", vmem=V7X_VMEM_LIMIT_BYTES_BIG_TILES),
        name="ffn",
    )(*args)
    return out if emit_bf16 else out[0]


def _inproj_kernel(x_ref, g_ref, w_ref, cw_ref, cg_ref, cinit_ref,
                   yc_ref, u_ref, cnew_ref, carry_ref, *, subs, blocks_per_seq, dc):
    i = pl.program_id(0)
    if blocks_per_seq > 1:
        @pl.when((i % blocks_per_seq) == 0)
        def _():
            carry_ref[0:2, :] = cinit_ref[0]

    xn = _rmsnorm(x_ref[...], g_ref[...]).astype(BF16)
    v = _dot(xn, w_ref[:, dc:2 * dc]) * _dot(xn, w_ref[:, 2 * dc:3 * dc])
    w0, w1, w2 = cw_ref[0:1, :], cw_ref[1:2, :], cw_ref[2:3, :]
    taps = []
    for r0, ts, s in subs:
        row = lax.broadcasted_iota(jnp.int32, (ts, 1), 0)
        vs = v[r0:r0 + ts]
        prev = carry_ref if blocks_per_seq > 1 else cinit_ref.at[s]
        p0, p1 = prev[0:1, :], prev[1:2, :]
        vm1 = jnp.where(row == 0, p1, pltpu.roll(vs, 1, 0))
        vm2 = jnp.where(row == 0, p0, jnp.where(row == 1, p1, pltpu.roll(vs, 2, 0)))
        taps.append(w0 * vm2 + w1 * vm1 + w2 * vs)
        tail = vs[ts - 2:ts]
        cnew_ref[s] = tail
        if blocks_per_seq > 1:
            carry_ref[0:2, :] = tail
    b_gate = _dot(xn, w_ref[:, 0:dc])
    for (r0, ts, s), tap in zip(subs, taps):
        yc_ref[r0:r0 + ts, :] = _rmsnorm(b_gate[r0:r0 + ts] * tap, cg_ref[...]).astype(yc_ref.dtype)
    u_ref[...] = _dot(xn, w_ref[:, 3 * dc:])


def _inproj(x, layer, g, w_in, conv_w, conv_g, conv_init, groups):
    m, d = x.shape
    n_seq = sum(nsq for nsq, _ in groups)
    dc = conv_w.shape[-1]
    ds = w_in.shape[-1] - 3 * dc
    if len(groups) == 1 and groups[0][1] >= INPROJ_ROW_TILE:
        seq_len = groups[0][1]
        tm = _row_tile(seq_len, INPROJ_ROW_TILE)
        ns, bps, subs = 1, seq_len // tm, ((0, tm, 0),)
    else:
        tm, ns, bps, subs, r0 = m, n_seq, 1, [], 0
        for nsq, seq_len in groups:
            for _ in range(nsq):
                subs.append((r0, seq_len, len(subs)))
                r0 += seq_len
        subs = tuple(subs)
    kern = functools.partial(_inproj_kernel, subs=subs, blocks_per_seq=bps, dc=dc)
    per_layer = lambda a: _resident((None,) + a.shape[1:], lambda i: (layer, 0, 0))
    return pl.pallas_call(
        kern,
        grid=(m // tm,),
        in_specs=[
            pl.BlockSpec((tm, d), lambda i: (i, 0)),
            per_layer(g), per_layer(w_in), per_layer(conv_w), per_layer(conv_g),
            pl.BlockSpec((ns, 2, dc), lambda i: (i // bps, 0, 0)),
        ],
        out_specs=[
            pl.BlockSpec((tm, dc), lambda i: (i, 0)),
            pl.BlockSpec((tm, ds), lambda i: (i, 0)),
            pl.BlockSpec((ns, 2, dc), lambda i: (i // bps, 0, 0)),
        ],
        out_shape=[
            jax.ShapeDtypeStruct((m, dc), BF16),
            jax.ShapeDtypeStruct((m, ds), F32),
            jax.ShapeDtypeStruct((n_seq, 2, dc), F32),
        ],
        scratch_shapes=[pltpu.VMEM((V7X_SUBLANES, dc), F32)],
        compiler_params=_params("arbitrary", vmem=V7X_VMEM_LIMIT_BYTES_BIG_TILES),
        name="inproj",
    )(x, g, w_in, conv_w, conv_g, conv_init)


def _cmul(ar, ai, br, bi):
    return ar * br - ai * bi, ar * bi + ai * br


def _scan_short(al_ref, init_ref, fin_ref, sl_ref, e_ref, *, npair, groups):
    for q in range(npair):
        a_r, a_i = al_ref[q, 0:1, :], al_ref[q, 1:2, :]
        for off, nb, nckb, s0 in groups:
            e_r, e_i = init_ref[q, 0, s0:s0 + nb, :], init_ref[q, 1, s0:s0 + nb, :]
            for c in range(nckb):
                rows = pl.ds(off + c, nb, stride=nckb)
                e_ref[2 * q, rows, :] = e_r
                e_ref[2 * q + 1, rows, :] = e_i
                e_r, e_i = _cmul(a_r, a_i, e_r, e_i)
                e_r, e_i = e_r + sl_ref[2 * q, rows, :], e_i + sl_ref[2 * q + 1, rows, :]
            fin_ref[q, 0, s0:s0 + nb, :] = e_r
            fin_ref[q, 1, s0:s0 + nb, :] = e_i


def _scan_in_vregs(q, al_ref, sl_ref, e_ref, v_ref, *, nck):
    n, sub = V7X_LANES, V7X_SUBLANES
    nv = nck // sub
    row8 = lax.broadcasted_iota(jnp.int32, (sub, n), 0)
    a_r, a_i = al_ref[q, 0:1, :], al_ref[q, 1:2, :]
    pw = [(jnp.ones_like(a_r), jnp.zeros_like(a_r)), (a_r, a_i)]
    for k in range(2, sub + 1):
        pw.append(_cmul(*pw[k // 2], *pw[k - k // 2]))
    v_r = sl_ref[2 * q].reshape(nv, sub, n)
    v_i = sl_ref[2 * q + 1].reshape(nv, sub, n)
    for k in (1, 2, 4):
        c_r = jnp.where(row8 >= k, pw[k][0], 0.0)[None]
        c_i = jnp.where(row8 >= k, pw[k][1], 0.0)[None]
        s_r, s_i = pltpu.roll(v_r, k, 1), pltpu.roll(v_i, k, 1)
        v_r, v_i = v_r + (c_r * s_r - c_i * s_i), v_i + (c_r * s_i + c_i * s_r)
    v_ref[2 * q] = v_r.reshape(nck, n)
    v_ref[2 * q + 1] = v_i.reshape(nck, n)
    keep = jnp.where(row8 >= 1, 1.0, 0.0)[None]
    e_ref[2 * q] = (pltpu.roll(v_r, 1, 1) * keep).reshape(nck, n)
    e_ref[2 * q + 1] = (pltpu.roll(v_i, 1, 1) * keep).reshape(nck, n)
    ent_r = jnp.zeros((sub, n), F32)
    ent_i = jnp.zeros((sub, n), F32)
    for r in range(sub):
        ent_r = jnp.where(row8 == r, pw[r][0], ent_r)
        ent_i = jnp.where(row8 == r, pw[r][1], ent_i)
    return (ent_r, ent_i, jnp.broadcast_to(pw[sub][0], (sub, n)),
            jnp.broadcast_to(pw[sub][1], (sub, n)))


def _scan_carry(vregs, state, tabs, e_ref, v_ref):
    n, sub = V7X_LANES, V7X_SUBLANES
    state = list(state)
    for k in vregs:
        rows = slice(k * sub, (k + 1) * sub)
        for q, (ent_r, ent_i, step_r, step_i) in enumerate(tabs):
            e0_r, e0_i = state[2 * q], state[2 * q + 1]
            c_r, c_i = _cmul(ent_r, ent_i, e0_r, e0_i)
            e_ref[2 * q, rows, :] += c_r
            e_ref[2 * q + 1, rows, :] += c_i
            last_r = jnp.broadcast_to(v_ref[2 * q, (k + 1) * sub - 1:(k + 1) * sub, :], (sub, n))
            last_i = jnp.broadcast_to(v_ref[2 * q + 1, (k + 1) * sub - 1:(k + 1) * sub, :], (sub, n))
            n_r, n_i = _cmul(step_r, step_i, e0_r, e0_i)
            state[2 * q], state[2 * q + 1] = n_r + last_r, n_i + last_i
    return state


def _ssm_kernel(u_ref, toep_ref, bm_ref, cmt_ref, dcat_ref, al_ref, init_ref, y_ref, fin_ref,
                sl_ref, e_ref, yin_ref, *scan_scratch, chunk, npair, nck, groups):
    n, sub = V7X_LANES, V7X_SUBLANES
    ucat = jnp.concatenate([u_ref[pl.ds(t, nck, stride=chunk), :] for t in range(chunk)], axis=1)
    ub = ucat.astype(BF16)
    sl = _dot(ub, bm_ref[...])
    for g in range(2 * npair):
        sl_ref[g] = sl[:, g * n:(g + 1) * n]

    def in_chunk(cols):
        yin_ref[:, cols] = _dot(ub, toep_ref[:, cols]) + dcat_ref[:, cols] * ucat[:, cols]

    def state_dot(rows):
        e = jnp.concatenate([e_ref[g, rows, :] for g in range(2 * npair)], axis=1).astype(BF16)
        return _dot_nt(e, cmt_ref[...])

    def finish(rows, ys):
        y = jax.nn.gelu(yin_ref[rows, :] + ys)
        nrows = rows.stop - rows.start
        for t in range(chunk):
            y_ref[pl.ds(rows.start * chunk + t, nrows, stride=chunk), :] = y[:, t * n:(t + 1) * n]

    if groups is not None:
        in_chunk(slice(0, chunk * n))
        _scan_short(al_ref, init_ref, fin_ref, sl_ref, e_ref, npair=npair, groups=groups)
        finish(slice(0, nck), state_dot(slice(0, nck)))
        return

    v_ref, st_ref = scan_scratch
    seg = pl.program_id(2)

    @pl.when(seg == 0)
    def _():
        for q in range(npair):
            st_ref[2 * q] = jnp.broadcast_to(init_ref[q, 0:1, :], (sub, n))
            st_ref[2 * q + 1] = jnp.broadcast_to(init_ref[q, 1:2, :], (sub, n))

    cw = chunk * n // npair
    tabs = []
    for q in range(npair):
        in_chunk(slice(q * cw, (q + 1) * cw))
        tabs.append(_scan_in_vregs(q, al_ref, sl_ref, e_ref, v_ref, nck=nck))
    nv = nck // sub
    parts = SSM_ROW_PARTS if nv % SSM_ROW_PARTS == 0 else 1
    state = [st_ref[i] for i in range(2 * npair)]
    pending = None
    for h in range(parts):
        state = _scan_carry(range(h * nv // parts, (h + 1) * nv // parts), state, tabs, e_ref, v_ref)
        if pending is not None:
            finish(*pending)
        rows = slice(h * nck // parts, (h + 1) * nck // parts)
        pending = (rows, state_dot(rows))
    finish(*pending)
    for i in range(2 * npair):
        st_ref[i] = state[i]

    @pl.when(seg == pl.num_programs(2) - 1)
    def _():
        for q in range(npair):
            fin_ref[q, 0:1, :] = state[2 * q][0:1, :]
            fin_ref[q, 1:2, :] = state[2 * q + 1][0:1, :]


def _long_scan(groups):
    return len(groups) == 1 and (groups[0][1] // SSM_CHUNK) % V7X_SUBLANES == 0


def _ssm(u, tabs, layer, init, groups):
    toep, bm, cmt, dcat, al = tabs
    m, ds = u.shape
    n = V7X_LANES
    chunk = SSM_CHUNK
    nj = ds // n
    npair_all = al.shape[1]
    npair = npair_all // nj
    n_seq = sum(nsq for nsq, _ in groups)
    assert all(seq_len % chunk == 0 for _, seq_len in groups)
    if _long_scan(groups):
        seq_len = groups[0][1]
        tseg = _row_tile(seq_len, SSM_SEG_TOKENS, unit=chunk * V7X_SUBLANES)
        nseg = seq_len // tseg
        nck, scan_groups = tseg // chunk, None
        grid = (nj, n_seq, nseg)
        sem = ("parallel", "parallel", "arbitrary")
        u_spec = pl.BlockSpec((tseg, n), lambda j, b, s: (b * nseg + s, j))
        st_spec = pl.BlockSpec((None, npair, 2, n), lambda j, b, s: (b, j, 0, 0))
        st_shape = (n_seq, npair_all, 2, n)
        tab = lambda r, c: pl.BlockSpec((None, None, r, c), lambda j, b, s: (layer, j, 0, 0))
        al_spec = pl.BlockSpec((None, npair, 2, n), lambda j, b, s: (layer, j, 0, 0))
        scan_scratch = [pltpu.VMEM((2 * npair, nck, n), F32),
                        pltpu.VMEM((2 * npair, V7X_SUBLANES, n), F32)]
    else:
        scan_groups, off, s0 = [], 0, 0
        for nsq, seq_len in groups:
            scan_groups.append((off, nsq, seq_len // chunk, s0))
            off += nsq * (seq_len // chunk)
            s0 += nsq
        nck, scan_groups = m // chunk, tuple(scan_groups)
        grid = (nj,)
        sem = ("parallel",)
        u_spec = pl.BlockSpec((m, n), lambda j: (0, j))
        st_spec = pl.BlockSpec((npair, 2, n_seq, n), lambda j: (j, 0, 0, 0))
        st_shape = (npair_all, 2, n_seq, n)
        tab = lambda r, c: pl.BlockSpec((None, None, r, c), lambda j: (layer, j, 0, 0))
        al_spec = pl.BlockSpec((None, npair, 2, n), lambda j: (layer, j, 0, 0))
        scan_scratch = []
    kern = functools.partial(_ssm_kernel, chunk=chunk, npair=npair, nck=nck, groups=scan_groups)
    return pl.pallas_call(
        kern,
        grid=grid,
        in_specs=[u_spec, tab(chunk * n, chunk * n), tab(chunk * n, 2 * npair * n),
                  tab(chunk * n, 2 * npair * n), tab(1, chunk * n), al_spec, st_spec],
        out_specs=[u_spec, st_spec],
        out_shape=[jax.ShapeDtypeStruct((m, ds), F32), jax.ShapeDtypeStruct(st_shape, F32)],
        scratch_shapes=[pltpu.VMEM((2 * npair, nck, n), F32)] * 2
        + [pltpu.VMEM((nck, chunk * n), F32)] + scan_scratch,
        compiler_params=_params(*sem),
        name="ssm",
    )(u, toep, bm, cmt, dcat, al, init)


def _ssm_prep_kernel(lr_ref, li_ref, ldt_ref, br_ref, bi_ref, cr_ref, ci_ref,
                     toep_ref, bm_ref, cmt_ref, al_ref, *, chunk, h):
    n = V7X_LANES
    npair = n // h // 2
    shift = h.bit_length() - 1
    lo = lax.broadcasted_iota(jnp.int32, (1, n), 1) < n // 2
    row_group = lax.broadcasted_iota(jnp.int32, (n, 1), 0) >> shift
    col_group = lax.broadcasted_iota(jnp.int32, (1, n), 1) >> shift
    pair_mask = [((row_group == 2 * q) & lo) | ((row_group == 2 * q + 1) & ~lo) for q in range(npair)]

    lr, li = lr_ref[...], li_ref[...]
    dt = jnp.exp(ldt_ref[...])
    mag = jnp.exp(lr * dt)
    ar, ai = mag * jnp.cos(li * dt), mag * jnp.sin(li * dt)
    den = lr * lr + li * li
    qr = ((ar - 1.0) * lr + ai * li) / den
    qi = (ai * lr - (ar - 1.0) * li) / den
    bbr, bbi = _cmul(qr, qi, br_ref[...], bi_ref[...])
    cr, ci = cr_ref[...], ci_ref[...]

    pw = [(jnp.ones_like(ar), jnp.zeros_like(ar))]
    for _ in range(chunk):
        pw.append(_cmul(*pw[-1], ar, ai))
    for q in range(npair):
        even, odd = 2 * q * h, (2 * q + 1) * h
        for part in range(2):
            al_ref[q, part:part + 1, :] = jnp.where(lo, pw[chunk][part][even:even + 1, :],
                                                    pw[chunk][part][odd:odd + 1, :])

    cpow = [_cmul(cr, ci, *pw[k]) for k in range(chunk + 1)]

    bb = jnp.where(lo, bbr, bbi)
    zeros = jnp.zeros((n, n), toep_ref.dtype)
    for k in range(chunk):
        kk = _dot_nt(bb, jnp.where(lo, cpow[k][0], -cpow[k][1]), precision=lax.Precision.HIGHEST)
        kk = jnp.where(row_group == col_group, kk, 0.0).astype(toep_ref.dtype)
        for s in range(chunk - k):
            toep_ref[s * n:(s + 1) * n, (s + k) * n:(s + k + 1) * n] = kk
        for s in range(k, chunk):
            if k > 0:
                toep_ref[s * n:(s + 1) * n, (s - k) * n:(s - k + 1) * n] = zeros
    for s in range(chunk):
        rows = slice(s * n, (s + 1) * n)
        er, ei = _cmul(*pw[chunk - 1 - s], bbr, bbi)
        for q in range(npair):
            c_re, c_im = slice(2 * q * n, (2 * q + 1) * n), slice((2 * q + 1) * n, (2 * q + 2) * n)
            bm_ref[rows, c_re] = jnp.where(pair_mask[q], er, 0.0).astype(bm_ref.dtype)
            bm_ref[rows, c_im] = jnp.where(pair_mask[q], ei, 0.0).astype(bm_ref.dtype)
            cmt_ref[rows, c_re] = jnp.where(pair_mask[q], cpow[s + 1][0], 0.0).astype(cmt_ref.dtype)
            cmt_ref[rows, c_im] = jnp.where(pair_mask[q], -cpow[s + 1][1], 0.0).astype(cmt_ref.dtype)


def _ssm_prep(lam_re, lam_im, log_dt, b_re, b_im, c_re, c_im, d_skip):
    nl, g, p = lam_re.shape
    h = d_skip.shape[-1]
    n = V7X_LANES
    chunk = SSM_CHUNK
    assert 2 * p == n and n % (2 * h) == 0 and h & (h - 1) == 0 and (g * h) % n == 0
    nj = g * h // n
    npair = n // h // 2
    twice = lambda a: jnp.concatenate([a, a], axis=-1)
    per_row = lambda a: jnp.repeat(twice(a), h, axis=1)
    lr, li = per_row(lam_re), per_row(lam_im)
    ldt = per_row(jnp.broadcast_to(log_dt[:, :, None], (nl, g, p)))
    rows = lambda a: twice(a).reshape(nl, g * h, n)
    br, bi = rows(jnp.swapaxes(b_re, -1, -2)), rows(jnp.swapaxes(b_im, -1, -2))
    cr, ci = rows(c_re), rows(c_im)
    in_spec = pl.BlockSpec((None, n, n), lambda l, j: (l, j, 0))
    tab = lambda r, c: pl.BlockSpec((None, None, r, c), lambda l, j: (l, j, 0, 0))
    toep, bm, cmt, al = pl.pallas_call(
        functools.partial(_ssm_prep_kernel, chunk=chunk, h=h),
        grid=(nl, nj),
        in_specs=[in_spec] * 7,
        out_specs=[tab(chunk * n, chunk * n), tab(chunk * n, 2 * npair * n),
                   tab(chunk * n, 2 * npair * n),
                   pl.BlockSpec((None, npair, 2, n), lambda l, j: (l, j, 0, 0))],
        out_shape=[
            jax.ShapeDtypeStruct((nl, nj, chunk * n, chunk * n), BF16),
            jax.ShapeDtypeStruct((nl, nj, chunk * n, 2 * npair * n), BF16),
            jax.ShapeDtypeStruct((nl, nj, chunk * n, 2 * npair * n), BF16),
            jax.ShapeDtypeStruct((nl, g // 2, 2, n), F32),
        ],
        compiler_params=_params("parallel", "parallel"),
        name="ssm_prep",
    )(lr, li, ldt, br, bi, cr, ci)
    dcat = jnp.tile(d_skip.reshape(nl, nj, 1, n), (1, 1, 1, chunk))
    return toep, bm, cmt, dcat, al


def _mixout_kernel(x_ref, yc_ref, ys_ref, glu_ref, sg_ref, wo_ref, o_ref, *, dc):
    y = ys_ref[...]
    z = y * jax.nn.sigmoid(_dot(y.astype(BF16), glu_ref[...]))
    zn = _rmsnorm(z, sg_ref[...]).astype(BF16)
    o_ref[...] = x_ref[...] + _dot(yc_ref[...], wo_ref[0:dc, :]) + _dot(zn, wo_ref[dc:, :])


def _mixout(x, yc, ys, layer, glu_w, ssm_g, w_out):
    m, d = x.shape
    dc, ds = yc.shape[1], ys.shape[1]
    tm = _row_tile(m, ROW_TILE)
    per_layer = lambda a: _resident((None,) + a.shape[1:], lambda i: (layer, 0, 0))
    return pl.pallas_call(
        functools.partial(_mixout_kernel, dc=dc),
        grid=(m // tm,),
        in_specs=[
            pl.BlockSpec((tm, d), lambda i: (i, 0)),
            pl.BlockSpec((tm, dc), lambda i: (i, 0)),
            pl.BlockSpec((tm, ds), lambda i: (i, 0)),
            per_layer(glu_w), per_layer(ssm_g), per_layer(w_out),
        ],
        out_specs=pl.BlockSpec((tm, d), lambda i: (i, 0)),
        out_shape=jax.ShapeDtypeStruct((m, d), F32),
        compiler_params=_params("parallel"),
        name="mixout",
    )(x, yc, ys, glu_w, ssm_g, w_out)


def _trunk(x, groups, conv_init, ssm_init, w, tabs, final_gain, ffn_bf16=None):
    depth = len(conv_init)
    emit = ffn_bf16 is None
    conv_out, ssm_out, casts = [], [], []
    for l in range(depth):
        fg = final_gain if l == depth - 1 else None
        if emit:
            x, *w1 = _ffn(x, w["ffn1_norm"], l, w["ffn1_w_gate"], w["ffn1_w_up"], w["ffn1_w_down"], l,
                          emit_bf16=True)
        else:
            x = _ffn(x, w["ffn1_norm"], l, *ffn_bf16[l][0], 0)
        yc, u, conv_new = _inproj(x, l, w["mix_norm"], w["w_in"], w["conv_w"], w["conv_out_norm"],
                                  conv_init[l], groups)
        ys, fin = _ssm(u, tabs, l, ssm_init[l], groups)
        x = _mixout(x, yc, ys, l, w["ssm_glu_w"], w["ssm_out_norm"], w["w_out"])
        if emit:
            x, *w2 = _ffn(x, w["ffn2_norm"], l, w["ffn2_w_gate"], w["ffn2_w_up"], w["ffn2_w_down"], l,
                          final_gain=fg, emit_bf16=True)
            casts.append((w1, w2))
        else:
            x = _ffn(x, w["ffn2_norm"], l, *ffn_bf16[l][1], 0, final_gain=fg)
        conv_out.append(conv_new)
        ssm_out.append(fin)
    return x, conv_out, ssm_out, casts


def kernel(x_prompt, x_sample, cache_conv, state_ssm_re, state_ssm_im, meta_tokens, ffn1_norm, ffn1_w_gate, ffn1_w_up, ffn1_w_down, mix_norm, w_in, conv_w, conv_out_norm, ssm_lambda_re, ssm_lambda_im, ssm_log_dt, ssm_b_re, ssm_b_im, ssm_c_re, ssm_c_im, ssm_d, ssm_glu_w, ssm_out_norm, w_out, ffn2_norm, ffn2_w_gate, ffn2_w_up, ffn2_w_down, final_norm):
    bsz, seq, d = x_prompt.shape
    dec_b, dec_t, _ = x_sample.shape
    n_meta = meta_tokens.shape[0]
    depth, g, p = ssm_lambda_re.shape
    dc = conv_w.shape[-1]
    n = V7X_LANES
    prompt_groups = ((bsz, seq),)
    small_groups = ((dec_b, dec_t), (1, n_meta))
    assert _long_scan(prompt_groups)

    tabs = _ssm_prep(ssm_lambda_re, ssm_lambda_im, ssm_log_dt, ssm_b_re, ssm_b_im,
                     ssm_c_re, ssm_c_im, ssm_d)
    cast = lambda a: a.astype(BF16)
    gain = lambda a: a[:, None, :]
    w = dict(
        ffn1_norm=gain(ffn1_norm), ffn1_w_gate=ffn1_w_gate, ffn1_w_up=ffn1_w_up,
        ffn1_w_down=ffn1_w_down, mix_norm=gain(mix_norm), w_in=cast(w_in), conv_w=conv_w,
        conv_out_norm=gain(conv_out_norm), ssm_glu_w=cast(ssm_glu_w),
        ssm_out_norm=gain(ssm_out_norm), w_out=cast(w_out), ffn2_norm=gain(ffn2_norm),
        ffn2_w_gate=ffn2_w_gate, ffn2_w_up=ffn2_w_up, ffn2_w_down=ffn2_w_down)
    final_gain = final_norm.reshape(1, d)

    n_dec = dec_b * dec_t
    x_small = jnp.concatenate([x_sample.reshape(n_dec, d), meta_tokens.astype(F32)], axis=0)
    pairs = lambda a: a.reshape(depth, -1, g // 2, n)
    s_init = jnp.transpose(jnp.stack([pairs(state_ssm_re), pairs(state_ssm_im)], axis=3),
                           (0, 2, 3, 1, 4))
    s_init = jnp.concatenate([s_init, jnp.zeros((depth, g // 2, 2, 1, n), F32)], axis=3)
    c_init = jnp.concatenate([cache_conv, jnp.zeros((depth, 1, 2, dc), F32)], axis=1)
    x_small, conv_sm, ssm_sm, ffn_bf16 = _trunk(
        x_small, small_groups, [c_init[l] for l in range(depth)],
        [s_init[l] for l in range(depth)], w, tabs, final_gain)

    conv_p0 = [jnp.broadcast_to(c[dec_b:], (bsz, 2, dc)) for c in conv_sm]
    ssm_p0 = [jnp.broadcast_to(s[None, :, :, dec_b], (bsz, g // 2, 2, n)) for s in ssm_sm]
    xp, conv_p, ssm_p, _ = _trunk(x_prompt.reshape(bsz * seq, d), prompt_groups, conv_p0, ssm_p0, w,
                                  tabs, final_gain, ffn_bf16)

    sp = jnp.stack(ssm_p)
    ss = jnp.transpose(jnp.stack(ssm_sm)[:, :, :, :dec_b], (0, 3, 1, 2, 4))
    part = lambda s, i: s[:, :, :, i, :].reshape(depth, -1, g, p)
    return (xp.reshape(bsz, seq, d), x_small[:n_dec].reshape(dec_b, dec_t, d), jnp.stack(conv_p),
            part(sp, 0), part(sp, 1), jnp.stack(conv_sm)[:, :dec_b], part(ss, 0), part(ss, 1))
```

```python
import functools

import jax
import jax.numpy as jnp
from jax import lax
from jax.experimental import pallas as pl
from jax.experimental.pallas import tpu as pltpu

F32 = jnp.float32
BF16 = jnp.bfloat16
EPS = 1e-6

V7X_LANES = 128
V7X_SUBLANES = 8
V7X_VMEM_LIMIT_BYTES = 56 * 1024 * 1024
V7X_VMEM_LIMIT_BYTES_BIG_TILES = 60 * 1024 * 1024

ROW_TILE = 512
INPROJ_ROW_TILE = 1024
INPROJ_ROW_PARTS = 2
FFN_ROW_TILE = 1024
FF_TILE = 512
FFN_EDGE_ROWS = 128
SSM_CHUNK = V7X_SUBLANES
SSM_SEG_TOKENS = 8192
SSM_ROW_PARTS = 4


def _row_tile(m, target, unit=V7X_SUBLANES):
    if m <= target:
        return m
    t = (target // unit) * unit
    while t > unit and m % t:
        t -= unit
    assert m % t == 0, (m, target)
    return t


def _lane_tile(n, target):
    t = (min(n, target) // V7X_LANES) * V7X_LANES
    while t > V7X_LANES and n % t:
        t -= V7X_LANES
    assert t > 0 and n % t == 0, (n, target)
    return t


def _params(*sem, vmem=V7X_VMEM_LIMIT_BYTES):
    return pltpu.CompilerParams(dimension_semantics=sem, vmem_limit_bytes=vmem)


def _resident(block_shape, index_map):
    return pl.BlockSpec(block_shape, index_map, pipeline_mode=pl.Buffered(1))


def _rmsnorm(x, g):
    ms = jnp.mean(x * x, axis=-1, keepdims=True)
    return x * lax.rsqrt(ms + EPS) * g


def _dot(a, b):
    return jnp.dot(a, b, preferred_element_type=F32)


def _dot_nt(a, b, precision=None):
    return lax.dot_general(a, b, (((1,), (1,)), ((), ())), preferred_element_type=F32,
                           precision=precision)


def _ffn_kernel(x_ref, g_ref, wg_ref, wu_ref, wd_ref, *rest, final_norm, emit_bf16):
    rest = list(rest)
    gf_ref = rest.pop(0) if final_norm else None
    o_ref = rest.pop(0)
    casts = [rest.pop(0) for _ in range(3)] if emit_bf16 else None
    xn_ref, = rest
    j = pl.program_id(1)
    tm = x_ref.shape[0]
    edge = _row_tile(tm, FFN_EDGE_ROWS, unit=2 * V7X_SUBLANES)

    def for_sub_blocks(fn, rolled=False):
        if rolled:
            def body(r, carry):
                fn(pl.ds(pl.multiple_of(r * edge, edge), edge))
                return carry
            lax.fori_loop(0, tm // edge, body, 0)
        else:
            for r in range(0, tm, edge):
                fn(pl.ds(r, edge))

    def step(first):
        xn = xn_ref[...]
        wg, wu, wd = (w[...].astype(BF16) for w in (wg_ref, wu_ref, wd_ref))
        if emit_bf16:
            for dst, w in zip(casts, (wg, wu, wd)):
                dst[...] = w
        h = (jax.nn.silu(_dot(xn, wg)) * _dot(xn, wu)).astype(BF16)
        if first:
            o_ref[...] = _dot(h, wd)
        else:
            o_ref[...] += _dot(h, wd)

    @pl.when(j == 0)
    def _():
        def norm_in(rows):
            xn_ref[rows, :] = _rmsnorm(x_ref[rows, :], g_ref[...]).astype(BF16)
        for_sub_blocks(norm_in)
        step(True)

    @pl.when(j > 0)
    def _():
        step(False)

    @pl.when(j == pl.num_programs(1) - 1)
    def _():
        def residual(rows):
            y = x_ref[rows, :] + 0.5 * o_ref[rows, :]
            o_ref[rows, :] = _rmsnorm(y, gf_ref[...]) if final_norm else y
        for_sub_blocks(residual, rolled=final_norm)


def _ffn(x, g, layer, wg, wu, wd, w_layer, final_gain=None, emit_bf16=False):
    m, d = x.shape
    f = wg.shape[-1]
    tm = _row_tile(m, FFN_ROW_TILE)
    tf = _lane_tile(f, FF_TILE)
    in_specs = [
        pl.BlockSpec((tm, d), lambda i, j: (i, 0)),
        pl.BlockSpec((None, 1, d), lambda i, j: (layer, 0, 0)),
        pl.BlockSpec((None, d, tf), lambda i, j: (w_layer, 0, j)),
        pl.BlockSpec((None, d, tf), lambda i, j: (w_layer, 0, j)),
        pl.BlockSpec((None, tf, d), lambda i, j: (w_layer, j, 0)),
    ]
    args = [x, g, wg, wu, wd]
    if final_gain is not None:
        in_specs.append(pl.BlockSpec((1, d), lambda i, j: (0, 0)))
        args.append(final_gain)
    out_specs = [pl.BlockSpec((tm, d), lambda i, j: (i, 0))]
    out_shape = [jax.ShapeDtypeStruct((m, d), F32)]
    if emit_bf16:
        assert m == tm, "each weight tile must be visited once"
        out_specs += [pl.BlockSpec((None, d, tf), lambda i, j: (0, 0, j))] * 2
        out_specs += [pl.BlockSpec((None, tf, d), lambda i, j: (0, j, 0))]
        out_shape += [jax.ShapeDtypeStruct((1, d, f), BF16)] * 2 + [jax.ShapeDtypeStruct((1, f, d), BF16)]
    out = pl.pallas_call(
        functools.partial(_ffn_kernel, final_norm=final_gain is not None, emit_bf16=emit_bf16),
        grid=(m // tm, f // tf),
        in_specs=in_specs,
        out_specs=out_specs,
        out_shape=out_shape,
        scratch_shapes=[pltpu.VMEM((tm, d), BF16)],
        compiler_params=_params("parallel", "arbitrary", vmem=V7X_VMEM_LIMIT_BYTES_BIG_TILES),
        name="ffn",
    )(*args)
    return out if emit_bf16 else out[0]


def _inproj_kernel(x_ref, g_ref, w_ref, cw_ref, cg_ref, cinit_ref,
                   yc_ref, u_ref, cnew_ref, carry_ref, *, parts, subs, blocks_per_seq, dc):
    i = pl.program_id(0)
    if blocks_per_seq > 1:
        @pl.when((i % blocks_per_seq) == 0)
        def _():
            carry_ref[0:2, :] = cinit_ref[0]

    w0, w1, w2 = cw_ref[0:1, :], cw_ref[1:2, :], cw_ref[2:3, :]
    for q0, qn in parts:
        xn = _rmsnorm(x_ref[q0:q0 + qn, :], g_ref[...]).astype(BF16)
        v = _dot(xn, w_ref[:, dc:2 * dc]) * _dot(xn, w_ref[:, 2 * dc:3 * dc])
        part_subs = [sub for sub in subs if q0 <= sub[0] < q0 + qn]
        taps = []
        for r0, ts, s in part_subs:
            row = lax.broadcasted_iota(jnp.int32, (ts, 1), 0)
            vs = v[r0 - q0:r0 - q0 + ts]
            prev = carry_ref if blocks_per_seq > 1 else cinit_ref.at[s]
            p0, p1 = prev[0:1, :], prev[1:2, :]
            vm1 = jnp.where(row == 0, p1, pltpu.roll(vs, 1, 0))
            vm2 = jnp.where(row == 0, p0, jnp.where(row == 1, p1, pltpu.roll(vs, 2, 0)))
            taps.append(w0 * vm2 + w1 * vm1 + w2 * vs)
            tail = vs[ts - 2:ts]
            cnew_ref[s] = tail
            if blocks_per_seq > 1:
                carry_ref[0:2, :] = tail
        b_gate = _dot(xn, w_ref[:, 0:dc])
        for (r0, ts, s), tap in zip(part_subs, taps):
            y = b_gate[r0 - q0:r0 - q0 + ts] * tap
            yc_ref[r0:r0 + ts, :] = _rmsnorm(y, cg_ref[...]).astype(yc_ref.dtype)
        u_ref[q0:q0 + qn, :] = _dot(xn, w_ref[:, 3 * dc:])


def _inproj(x, layer, g, w_in, conv_w, conv_g, conv_init, groups):
    m, d = x.shape
    n_seq = sum(nsq for nsq, _ in groups)
    dc = conv_w.shape[-1]
    ds = w_in.shape[-1] - 3 * dc
    if len(groups) == 1 and groups[0][1] >= INPROJ_ROW_TILE:
        seq_len = groups[0][1]
        tm = _row_tile(seq_len, INPROJ_ROW_TILE)
        ns, bps = 1, seq_len // tm
        npart = INPROJ_ROW_PARTS if tm % (INPROJ_ROW_PARTS * 2 * V7X_SUBLANES) == 0 else 1
        parts = tuple((r, tm // npart) for r in range(0, tm, tm // npart))
        subs = tuple((r, rows, 0) for r, rows in parts)
    else:
        tm, ns, bps, subs, r0 = m, n_seq, 1, [], 0
        for nsq, seq_len in groups:
            for _ in range(nsq):
                subs.append((r0, seq_len, len(subs)))
                r0 += seq_len
        subs, parts = tuple(subs), ((0, m),)
    kern = functools.partial(_inproj_kernel, parts=parts, subs=subs, blocks_per_seq=bps, dc=dc)
    per_layer = lambda a: _resident((None,) + a.shape[1:], lambda i: (layer, 0, 0))
    return pl.pallas_call(
        kern,
        grid=(m // tm,),
        in_specs=[
            pl.BlockSpec((tm, d), lambda i: (i, 0)),
            per_layer(g), per_layer(w_in), per_layer(conv_w), per_layer(conv_g),
            pl.BlockSpec((ns, 2, dc), lambda i: (i // bps, 0, 0)),
        ],
        out_specs=[
            pl.BlockSpec((tm, dc), lambda i: (i, 0)),
            pl.BlockSpec((tm, ds), lambda i: (i, 0)),
            pl.BlockSpec((ns, 2, dc), lambda i: (i // bps, 0, 0)),
        ],
        out_shape=[
            jax.ShapeDtypeStruct((m, dc), BF16),
            jax.ShapeDtypeStruct((m, ds), F32),
            jax.ShapeDtypeStruct((n_seq, 2, dc), F32),
        ],
        scratch_shapes=[pltpu.VMEM((V7X_SUBLANES, dc), F32)],
        compiler_params=_params("arbitrary", vmem=V7X_VMEM_LIMIT_BYTES_BIG_TILES),
        name="inproj",
    )(x, g, w_in, conv_w, conv_g, conv_init)


def _cmul(ar, ai, br, bi):
    return ar * br - ai * bi, ar * bi + ai * br


def _scan_short(al_ref, init_ref, fin_ref, sl_ref, e_ref, *, npair, groups):
    for q in range(npair):
        a_r, a_i = al_ref[q, 0:1, :], al_ref[q, 1:2, :]
        for off, nb, nckb, s0 in groups:
            e_r, e_i = init_ref[q, 0, s0:s0 + nb, :], init_ref[q, 1, s0:s0 + nb, :]
            for c in range(nckb):
                rows = pl.ds(off + c, nb, stride=nckb)
                e_ref[2 * q, rows, :] = e_r
                e_ref[2 * q + 1, rows, :] = e_i
                e_r, e_i = _cmul(a_r, a_i, e_r, e_i)
                e_r, e_i = e_r + sl_ref[2 * q, rows, :], e_i + sl_ref[2 * q + 1, rows, :]
            fin_ref[q, 0, s0:s0 + nb, :] = e_r
            fin_ref[q, 1, s0:s0 + nb, :] = e_i


def _scan_tables(q, al_ref):
    n, sub = V7X_LANES, V7X_SUBLANES
    row8 = lax.broadcasted_iota(jnp.int32, (sub, n), 0)
    a_r, a_i = al_ref[q, 0:1, :], al_ref[q, 1:2, :]
    pw = [(jnp.ones_like(a_r), jnp.zeros_like(a_r)), (a_r, a_i)]
    for k in range(2, sub + 1):
        pw.append(_cmul(*pw[k // 2], *pw[k - k // 2]))
    levels = [(jnp.where(row8 >= k, pw[k][0], 0.0), jnp.where(row8 >= k, pw[k][1], 0.0))
              for k in (1, 2, 4)]
    ent_r = jnp.zeros((sub, n), F32)
    ent_i = jnp.zeros((sub, n), F32)
    for r in range(sub):
        ent_r = jnp.where(row8 == r, pw[r + 1][0], ent_r)
        ent_i = jnp.where(row8 == r, pw[r + 1][1], ent_i)
    return levels, ent_r, ent_i


def _scan_vregs(vregs, state, tabs, sl_ref, e_ref):
    n, sub = V7X_LANES, V7X_SUBLANES
    first = lax.broadcasted_iota(jnp.int32, (sub, n), 0) == 0
    state = list(state)
    for k in vregs:
        rows = slice(k * sub, (k + 1) * sub)
        for q, (levels, ent_r, ent_i) in enumerate(tabs):
            v_r, v_i = sl_ref[2 * q, rows, :], sl_ref[2 * q + 1, rows, :]
            for shift, (c_r, c_i) in zip((1, 2, 4), levels):
                s_r, s_i = pltpu.roll(v_r, shift, 0), pltpu.roll(v_i, shift, 0)
                v_r, v_i = v_r + (c_r * s_r - c_i * s_i), v_i + (c_r * s_i + c_i * s_r)
            e0_r, e0_i = state[2 * q], state[2 * q + 1]
            g_r, g_i = _cmul(ent_r, ent_i, e0_r, e0_i)
            w_r, w_i = v_r + g_r, v_i + g_i
            e_ref[2 * q, rows, :] = jnp.where(first, e0_r, pltpu.roll(w_r, 1, 0))
            e_ref[2 * q + 1, rows, :] = jnp.where(first, e0_i, pltpu.roll(w_i, 1, 0))
            state[2 * q] = jnp.broadcast_to(w_r[sub - 1:sub, :], (sub, n))
            state[2 * q + 1] = jnp.broadcast_to(w_i[sub - 1:sub, :], (sub, n))
    return state


def _ssm_kernel(u_ref, toep_ref, bm_ref, cmt_ref, dcat_ref, al_ref, init_ref, y_ref, fin_ref,
                sl_ref, e_ref, yin_ref, *scan_scratch, chunk, npair, nck, groups):
    n, sub = V7X_LANES, V7X_SUBLANES
    ucat = jnp.concatenate([u_ref[pl.ds(t, nck, stride=chunk), :] for t in range(chunk)], axis=1)
    ub = ucat.astype(BF16)
    sl = _dot(ub, bm_ref[...])
    for g in range(2 * npair):
        sl_ref[g] = sl[:, g * n:(g + 1) * n]

    def in_chunk(cols):
        yin_ref[:, cols] = _dot(ub, toep_ref[:, cols]) + dcat_ref[:, cols] * ucat[:, cols]

    def state_dot(rows):
        e = jnp.concatenate([e_ref[g, rows, :] for g in range(2 * npair)], axis=1).astype(BF16)
        return _dot_nt(e, cmt_ref[...])

    def finish(rows, ys):
        y = jax.nn.gelu(yin_ref[rows, :] + ys)
        nrows = rows.stop - rows.start
        for t in range(chunk):
            y_ref[pl.ds(rows.start * chunk + t, nrows, stride=chunk), :] = y[:, t * n:(t + 1) * n]

    if groups is not None:
        in_chunk(slice(0, chunk * n))
        _scan_short(al_ref, init_ref, fin_ref, sl_ref, e_ref, npair=npair, groups=groups)
        finish(slice(0, nck), state_dot(slice(0, nck)))
        return

    st_ref, = scan_scratch
    seg = pl.program_id(2)

    @pl.when(seg == 0)
    def _():
        for q in range(npair):
            st_ref[2 * q] = jnp.broadcast_to(init_ref[q, 0:1, :], (sub, n))
            st_ref[2 * q + 1] = jnp.broadcast_to(init_ref[q, 1:2, :], (sub, n))

    tabs = [_scan_tables(q, al_ref) for q in range(npair)]
    nv = nck // sub
    parts = SSM_ROW_PARTS if nv % SSM_ROW_PARTS == 0 else 1
    spread = min(parts, 2)
    nslice = 4 if (chunk * n) % (4 * 2 * n) == 0 else spread
    cw = chunk * n // nslice
    col_slices = [slice(q * cw, (q + 1) * cw) for q in range(nslice)]
    state = [st_ref[i] for i in range(2 * npair)]
    pending = None
    for h in range(parts):
        for cols in col_slices[h * nslice // spread:(h + 1) * nslice // spread] if h < spread else []:
            in_chunk(cols)
        state = _scan_vregs(range(h * nv // parts, (h + 1) * nv // parts), state, tabs, sl_ref, e_ref)
        if pending is not None:
            finish(*pending)
        rows = slice(h * nck // parts, (h + 1) * nck // parts)
        pending = (rows, state_dot(rows))
    finish(*pending)
    for i in range(2 * npair):
        st_ref[i] = state[i]

    @pl.when(seg == pl.num_programs(2) - 1)
    def _():
        for q in range(npair):
            fin_ref[q, 0:1, :] = state[2 * q][0:1, :]
            fin_ref[q, 1:2, :] = state[2 * q + 1][0:1, :]


def _long_scan(groups):
    return len(groups) == 1 and (groups[0][1] // SSM_CHUNK) % V7X_SUBLANES == 0


def _ssm(u, tabs, layer, init, groups):
    toep, bm, cmt, dcat, al = tabs
    m, ds = u.shape
    n = V7X_LANES
    chunk = SSM_CHUNK
    nj = ds // n
    npair_all = al.shape[1]
    npair = npair_all // nj
    n_seq = sum(nsq for nsq, _ in groups)
    assert all(seq_len % chunk == 0 for _, seq_len in groups)
    if _long_scan(groups):
        seq_len = groups[0][1]
        tseg = _row_tile(seq_len, SSM_SEG_TOKENS, unit=chunk * V7X_SUBLANES)
        nseg = seq_len // tseg
        nck, scan_groups = tseg // chunk, None
        grid = (nj, n_seq, nseg)
        sem = ("parallel", "parallel", "arbitrary")
        u_spec = pl.BlockSpec((tseg, n), lambda j, b, s: (b * nseg + s, j))
        st_spec = pl.BlockSpec((None, npair, 2, n), lambda j, b, s: (b, j, 0, 0))
        st_shape = (n_seq, npair_all, 2, n)
        tab = lambda r, c: pl.BlockSpec((None, None, r, c), lambda j, b, s: (layer, j, 0, 0))
        al_spec = pl.BlockSpec((None, npair, 2, n), lambda j, b, s: (layer, j, 0, 0))
        scan_scratch = [pltpu.VMEM((2 * npair, V7X_SUBLANES, n), F32)]
    else:
        scan_groups, off, s0 = [], 0, 0
        for nsq, seq_len in groups:
            scan_groups.append((off, nsq, seq_len // chunk, s0))
            off += nsq * (seq_len // chunk)
            s0 += nsq
        nck, scan_groups = m // chunk, tuple(scan_groups)
        grid = (nj,)
        sem = ("parallel",)
        u_spec = pl.BlockSpec((m, n), lambda j: (0, j))
        st_spec = pl.BlockSpec((npair, 2, n_seq, n), lambda j: (j, 0, 0, 0))
        st_shape = (npair_all, 2, n_seq, n)
        tab = lambda r, c: pl.BlockSpec((None, None, r, c), lambda j: (layer, j, 0, 0))
        al_spec = pl.BlockSpec((None, npair, 2, n), lambda j: (layer, j, 0, 0))
        scan_scratch = []
    kern = functools.partial(_ssm_kernel, chunk=chunk, npair=npair, nck=nck, groups=scan_groups)
    return pl.pallas_call(
        kern,
        grid=grid,
        in_specs=[u_spec, tab(chunk * n, chunk * n), tab(chunk * n, 2 * npair * n),
                  tab(chunk * n, 2 * npair * n), tab(1, chunk * n), al_spec, st_spec],
        out_specs=[u_spec, st_spec],
        out_shape=[jax.ShapeDtypeStruct((m, ds), F32), jax.ShapeDtypeStruct(st_shape, F32)],
        scratch_shapes=[pltpu.VMEM((2 * npair, nck, n), F32)] * 2
        + [pltpu.VMEM((nck, chunk * n), F32)] + scan_scratch,
        compiler_params=_params(*sem),
        name="ssm",
    )(u, toep, bm, cmt, dcat, al, init)


def _ssm_prep_kernel(lr_ref, li_ref, ldt_ref, br_ref, bi_ref, cr_ref, ci_ref,
                     toep_ref, bm_ref, cmt_ref, al_ref, *, chunk, h):
    n = V7X_LANES
    npair = n // h // 2
    shift = h.bit_length() - 1
    lo = lax.broadcasted_iota(jnp.int32, (1, n), 1) < n // 2
    row_group = lax.broadcasted_iota(jnp.int32, (n, 1), 0) >> shift
    col_group = lax.broadcasted_iota(jnp.int32, (1, n), 1) >> shift
    pair_mask = [((row_group == 2 * q) & lo) | ((row_group == 2 * q + 1) & ~lo) for q in range(npair)]

    lr, li = lr_ref[...], li_ref[...]
    dt = jnp.exp(ldt_ref[...])
    mag = jnp.exp(lr * dt)
    ar, ai = mag * jnp.cos(li * dt), mag * jnp.sin(li * dt)
    den = lr * lr + li * li
    qr = ((ar - 1.0) * lr + ai * li) / den
    qi = (ai * lr - (ar - 1.0) * li) / den
    bbr, bbi = _cmul(qr, qi, br_ref[...], bi_ref[...])
    cr, ci = cr_ref[...], ci_ref[...]

    pw = [(jnp.ones_like(ar), jnp.zeros_like(ar))]
    for _ in range(chunk):
        pw.append(_cmul(*pw[-1], ar, ai))
    for q in range(npair):
        even, odd = 2 * q * h, (2 * q + 1) * h
        for part in range(2):
            al_ref[q, part:part + 1, :] = jnp.where(lo, pw[chunk][part][even:even + 1, :],
                                                    pw[chunk][part][odd:odd + 1, :])

    cpow = [_cmul(cr, ci, *pw[k]) for k in range(chunk + 1)]

    bb = jnp.where(lo, bbr, bbi)
    zeros = jnp.zeros((n, n), toep_ref.dtype)
    for k in range(chunk):
        kk = _dot_nt(bb, jnp.where(lo, cpow[k][0], -cpow[k][1]), precision=lax.Precision.HIGHEST)
        kk = jnp.where(row_group == col_group, kk, 0.0).astype(toep_ref.dtype)
        for s in range(chunk - k):
            toep_ref[s * n:(s + 1) * n, (s + k) * n:(s + k + 1) * n] = kk
        for s in range(k, chunk):
            if k > 0:
                toep_ref[s * n:(s + 1) * n, (s - k) * n:(s - k + 1) * n] = zeros
    for s in range(chunk):
        rows = slice(s * n, (s + 1) * n)
        er, ei = _cmul(*pw[chunk - 1 - s], bbr, bbi)
        for q in range(npair):
            c_re, c_im = slice(2 * q * n, (2 * q + 1) * n), slice((2 * q + 1) * n, (2 * q + 2) * n)
            bm_ref[rows, c_re] = jnp.where(pair_mask[q], er, 0.0).astype(bm_ref.dtype)
            bm_ref[rows, c_im] = jnp.where(pair_mask[q], ei, 0.0).astype(bm_ref.dtype)
            cmt_ref[rows, c_re] = jnp.where(pair_mask[q], cpow[s + 1][0], 0.0).astype(cmt_ref.dtype)
            cmt_ref[rows, c_im] = jnp.where(pair_mask[q], -cpow[s + 1][1], 0.0).astype(cmt_ref.dtype)


def _ssm_prep(lam_re, lam_im, log_dt, b_re, b_im, c_re, c_im, d_skip):
    nl, g, p = lam_re.shape
    h = d_skip.shape[-1]
    n = V7X_LANES
    chunk = SSM_CHUNK
    assert 2 * p == n and n % (2 * h) == 0 and h & (h - 1) == 0 and (g * h) % n == 0
    nj = g * h // n
    npair = n // h // 2
    twice = lambda a: jnp.concatenate([a, a], axis=-1)
    per_row = lambda a: jnp.repeat(twice(a), h, axis=1)
    lr, li = per_row(lam_re), per_row(lam_im)
    ldt = per_row(jnp.broadcast_to(log_dt[:, :, None], (nl, g, p)))
    rows = lambda a: twice(a).reshape(nl, g * h, n)
    br, bi = rows(jnp.swapaxes(b_re, -1, -2)), rows(jnp.swapaxes(b_im, -1, -2))
    cr, ci = rows(c_re), rows(c_im)
    in_spec = pl.BlockSpec((None, n, n), lambda l, j: (l, j, 0))
    tab = lambda r, c: pl.BlockSpec((None, None, r, c), lambda l, j: (l, j, 0, 0))
    toep, bm, cmt, al = pl.pallas_call(
        functools.partial(_ssm_prep_kernel, chunk=chunk, h=h),
        grid=(nl, nj),
        in_specs=[in_spec] * 7,
        out_specs=[tab(chunk * n, chunk * n), tab(chunk * n, 2 * npair * n),
                   tab(chunk * n, 2 * npair * n),
                   pl.BlockSpec((None, npair, 2, n), lambda l, j: (l, j, 0, 0))],
        out_shape=[
            jax.ShapeDtypeStruct((nl, nj, chunk * n, chunk * n), BF16),
            jax.ShapeDtypeStruct((nl, nj, chunk * n, 2 * npair * n), BF16),
            jax.ShapeDtypeStruct((nl, nj, chunk * n, 2 * npair * n), BF16),
            jax.ShapeDtypeStruct((nl, g // 2, 2, n), F32),
        ],
        compiler_params=_params("parallel", "parallel"),
        name="ssm_prep",
    )(lr, li, ldt, br, bi, cr, ci)
    dcat = jnp.tile(d_skip.reshape(nl, nj, 1, n), (1, 1, 1, chunk))
    return toep, bm, cmt, dcat, al


def _mixout_kernel(x_ref, yc_ref, ys_ref, glu_ref, sg_ref, wo_ref, o_ref, *, dc):
    y = ys_ref[...]
    z = y * jax.nn.sigmoid(_dot(y.astype(BF16), glu_ref[...]))
    zn = _rmsnorm(z, sg_ref[...]).astype(BF16)
    o_ref[...] = x_ref[...] + _dot(yc_ref[...], wo_ref[0:dc, :]) + _dot(zn, wo_ref[dc:, :])


def _mixout(x, yc, ys, layer, glu_w, ssm_g, w_out):
    m, d = x.shape
    dc, ds = yc.shape[1], ys.shape[1]
    tm = _row_tile(m, ROW_TILE)
    per_layer = lambda a: _resident((None,) + a.shape[1:], lambda i: (layer, 0, 0))
    return pl.pallas_call(
        functools.partial(_mixout_kernel, dc=dc),
        grid=(m // tm,),
        in_specs=[
            pl.BlockSpec((tm, d), lambda i: (i, 0)),
            pl.BlockSpec((tm, dc), lambda i: (i, 0)),
            pl.BlockSpec((tm, ds), lambda i: (i, 0)),
            per_layer(glu_w), per_layer(ssm_g), per_layer(w_out),
        ],
        out_specs=pl.BlockSpec((tm, d), lambda i: (i, 0)),
        out_shape=jax.ShapeDtypeStruct((m, d), F32),
        compiler_params=_params("parallel"),
        name="mixout",
    )(x, yc, ys, glu_w, ssm_g, w_out)


def _trunk(x, groups, conv_init, ssm_init, w, tabs, final_gain, ffn_bf16=None):
    depth = len(conv_init)
    emit = ffn_bf16 is None
    conv_out, ssm_out, casts = [], [], []
    for l in range(depth):
        fg = final_gain if l == depth - 1 else None
        if emit:
            x, *w1 = _ffn(x, w["ffn1_norm"], l, w["ffn1_w_gate"], w["ffn1_w_up"], w["ffn1_w_down"], l,
                          emit_bf16=True)
        else:
            x = _ffn(x, w["ffn1_norm"], l, *ffn_bf16[l][0], 0)
        yc, u, conv_new = _inproj(x, l, w["mix_norm"], w["w_in"], w["conv_w"], w["conv_out_norm"],
                                  conv_init[l], groups)
        ys, fin = _ssm(u, tabs, l, ssm_init[l], groups)
        x = _mixout(x, yc, ys, l, w["ssm_glu_w"], w["ssm_out_norm"], w["w_out"])
        if emit:
            x, *w2 = _ffn(x, w["ffn2_norm"], l, w["ffn2_w_gate"], w["ffn2_w_up"], w["ffn2_w_down"], l,
                          final_gain=fg, emit_bf16=True)
            casts.append((w1, w2))
        else:
            x = _ffn(x, w["ffn2_norm"], l, *ffn_bf16[l][1], 0, final_gain=fg)
        conv_out.append(conv_new)
        ssm_out.append(fin)
    return x, conv_out, ssm_out, casts


def kernel(x_prompt, x_sample, cache_conv, state_ssm_re, state_ssm_im, meta_tokens, ffn1_norm, ffn1_w_gate, ffn1_w_up, ffn1_w_down, mix_norm, w_in, conv_w, conv_out_norm, ssm_lambda_re, ssm_lambda_im, ssm_log_dt, ssm_b_re, ssm_b_im, ssm_c_re, ssm_c_im, ssm_d, ssm_glu_w, ssm_out_norm, w_out, ffn2_norm, ffn2_w_gate, ffn2_w_up, ffn2_w_down, final_norm):
    bsz, seq, d = x_prompt.shape
    dec_b, dec_t, _ = x_sample.shape
    n_meta = meta_tokens.shape[0]
    depth, g, p = ssm_lambda_re.shape
    dc = conv_w.shape[-1]
    n = V7X_LANES
    prompt_groups = ((bsz, seq),)
    small_groups = ((dec_b, dec_t), (1, n_meta))
    assert _long_scan(prompt_groups)

    tabs = _ssm_prep(ssm_lambda_re, ssm_lambda_im, ssm_log_dt, ssm_b_re, ssm_b_im,
                     ssm_c_re, ssm_c_im, ssm_d)
    cast = lambda a: a.astype(BF16)
    gain = lambda a: a[:, None, :]
    w = dict(
        ffn1_norm=gain(ffn1_norm), ffn1_w_gate=ffn1_w_gate, ffn1_w_up=ffn1_w_up,
        ffn1_w_down=ffn1_w_down, mix_norm=gain(mix_norm), w_in=cast(w_in), conv_w=conv_w,
        conv_out_norm=gain(conv_out_norm), ssm_glu_w=cast(ssm_glu_w),
        ssm_out_norm=gain(ssm_out_norm), w_out=cast(w_out), ffn2_norm=gain(ffn2_norm),
        ffn2_w_gate=ffn2_w_gate, ffn2_w_up=ffn2_w_up, ffn2_w_down=ffn2_w_down)
    final_gain = final_norm.reshape(1, d)

    n_dec = dec_b * dec_t
    x_small = jnp.concatenate([x_sample.reshape(n_dec, d), meta_tokens.astype(F32)], axis=0)
    pairs = lambda a: a.reshape(depth, -1, g // 2, n)
    s_init = jnp.transpose(jnp.stack([pairs(state_ssm_re), pairs(state_ssm_im)], axis=3),
                           (0, 2, 3, 1, 4))
    s_init = jnp.concatenate([s_init, jnp.zeros((depth, g // 2, 2, 1, n), F32)], axis=3)
    c_init = jnp.concatenate([cache_conv, jnp.zeros((depth, 1, 2, dc), F32)], axis=1)
    x_small, conv_sm, ssm_sm, ffn_bf16 = _trunk(
        x_small, small_groups, [c_init[l] for l in range(depth)],
        [s_init[l] for l in range(depth)], w, tabs, final_gain)

    conv_p0 = [jnp.broadcast_to(c[dec_b:], (bsz, 2, dc)) for c in conv_sm]
    ssm_p0 = [jnp.broadcast_to(s[None, :, :, dec_b], (bsz, g // 2, 2, n)) for s in ssm_sm]
    xp, conv_p, ssm_p, _ = _trunk(x_prompt.reshape(bsz * seq, d), prompt_groups, conv_p0, ssm_p0, w,
                                  tabs, final_gain, ffn_bf16)

    sp = jnp.stack(ssm_p)
    ss = jnp.transpose(jnp.stack(ssm_sm)[:, :, :, :dec_b], (0, 3, 1, 2, 4))
    part = lambda s, i: s[:, :, :, i, :].reshape(depth, -1, g, p)
    return (xp.reshape(bsz, seq, d), x_small[:n_dec].reshape(dec_b, dec_t, d), jnp.stack(conv_p),
            part(sp, 0), part(sp, 1), jnp.stack(conv_sm)[:, :dec_b], part(ss, 0), part(ss, 1))
```

```python
import functools

import jax
import jax.numpy as jnp
from jax import lax
from jax.experimental import pallas as pl
from jax.experimental.pallas import tpu as pltpu

F32 = jnp.float32
BF16 = jnp.bfloat16
EPS = 1e-6

V7X_LANES = 128
V7X_SUBLANES = 8
V7X_VMEM_LIMIT_BYTES = 56 * 1024 * 1024
V7X_VMEM_LIMIT_BYTES_BIG_TILES = 60 * 1024 * 1024

ROW_TILE = 512
INPROJ_ROW_TILE = 1024
INPROJ_ROW_PARTS = 2
FFN_ROW_TILE = 1024
FF_TILE = 512
FFN_EDGE_ROWS = 128
SSM_CHUNK = V7X_SUBLANES
SSM_SEG_TOKENS = 8192
SSM_ROW_PARTS = 4


def _row_tile(m, target, unit=V7X_SUBLANES):
    if m <= target:
        return m
    t = (target // unit) * unit
    while t > unit and m % t:
        t -= unit
    assert m % t == 0, (m, target)
    return t


def _lane_tile(n, target):
    t = (min(n, target) // V7X_LANES) * V7X_LANES
    while t > V7X_LANES and n % t:
        t -= V7X_LANES
    assert t > 0 and n % t == 0, (n, target)
    return t


def _params(*sem, vmem=V7X_VMEM_LIMIT_BYTES):
    return pltpu.CompilerParams(dimension_semantics=sem, vmem_limit_bytes=vmem)


def _resident(block_shape, index_map):
    return pl.BlockSpec(block_shape, index_map, pipeline_mode=pl.Buffered(1))


def _rmsnorm(x, g):
    ms = jnp.mean(x * x, axis=-1, keepdims=True)
    return x * lax.rsqrt(ms + EPS) * g


def _dot(a, b):
    return jnp.dot(a, b, preferred_element_type=F32)


def _dot_nt(a, b, precision=None):
    return lax.dot_general(a, b, (((1,), (1,)), ((), ())), preferred_element_type=F32,
                           precision=precision)


def _ffn_kernel(x_ref, g_ref, wg_ref, wu_ref, wd_ref, *rest, final_norm, emit_bf16):
    rest = list(rest)
    gf_ref = rest.pop(0) if final_norm else None
    o_ref = rest.pop(0)
    casts = [rest.pop(0) for _ in range(3)] if emit_bf16 else None
    xn_ref, = rest
    j = pl.program_id(1)
    tm = x_ref.shape[0]
    edge = _row_tile(tm, FFN_EDGE_ROWS, unit=2 * V7X_SUBLANES)

    def for_sub_blocks(fn, rolled=False):
        if rolled:
            def body(r, carry):
                fn(pl.ds(pl.multiple_of(r * edge, edge), edge))
                return carry
            lax.fori_loop(0, tm // edge, body, 0)
        else:
            for r in range(0, tm, edge):
                fn(pl.ds(r, edge))

    def step(first):
        xn = xn_ref[...]
        wg, wu = wg_ref[...].astype(BF16), wu_ref[...].astype(BF16)
        wd = (wd_ref[...] * 0.5 if emit_bf16 else wd_ref[...]).astype(BF16)
        if emit_bf16:
            for dst, w in zip(casts, (wg, wu, wd)):
                dst[...] = w
        h = (jax.nn.silu(_dot(xn, wg)) * _dot(xn, wu)).astype(BF16)
        if first:
            o_ref[...] = x_ref[...] + _dot(h, wd)
        else:
            o_ref[...] += _dot(h, wd)

    @pl.when(j == 0)
    def _():
        def norm_in(rows):
            xn_ref[rows, :] = _rmsnorm(x_ref[rows, :], g_ref[...]).astype(BF16)
        for_sub_blocks(norm_in)
        step(True)

    @pl.when(j > 0)
    def _():
        step(False)

    if final_norm:
        @pl.when(j == pl.num_programs(1) - 1)
        def _():
            def last_norm(rows):
                o_ref[rows, :] = _rmsnorm(o_ref[rows, :], gf_ref[...])
            for_sub_blocks(last_norm, rolled=True)


def _ffn(x, g, layer, wg, wu, wd, w_layer, final_gain=None, emit_bf16=False):
    m, d = x.shape
    f = wg.shape[-1]
    tm = _row_tile(m, FFN_ROW_TILE)
    tf = _lane_tile(f, FF_TILE)
    in_specs = [
        pl.BlockSpec((tm, d), lambda i, j: (i, 0)),
        pl.BlockSpec((None, 1, d), lambda i, j: (layer, 0, 0)),
        pl.BlockSpec((None, d, tf), lambda i, j: (w_layer, 0, j)),
        pl.BlockSpec((None, d, tf), lambda i, j: (w_layer, 0, j)),
        pl.BlockSpec((None, tf, d), lambda i, j: (w_layer, j, 0)),
    ]
    args = [x, g, wg, wu, wd]
    if final_gain is not None:
        in_specs.append(pl.BlockSpec((1, d), lambda i, j: (0, 0)))
        args.append(final_gain)
    out_specs = [pl.BlockSpec((tm, d), lambda i, j: (i, 0))]
    out_shape = [jax.ShapeDtypeStruct((m, d), F32)]
    if emit_bf16:
        assert m == tm, "each weight tile must be visited once"
        out_specs += [pl.BlockSpec((None, d, tf), lambda i, j: (0, 0, j))] * 2
        out_specs += [pl.BlockSpec((None, tf, d), lambda i, j: (0, j, 0))]
        out_shape += [jax.ShapeDtypeStruct((1, d, f), BF16)] * 2 + [jax.ShapeDtypeStruct((1, f, d), BF16)]
    out = pl.pallas_call(
        functools.partial(_ffn_kernel, final_norm=final_gain is not None, emit_bf16=emit_bf16),
        grid=(m // tm, f // tf),
        in_specs=in_specs,
        out_specs=out_specs,
        out_shape=out_shape,
        scratch_shapes=[pltpu.VMEM((tm, d), BF16)],
        compiler_params=_params("parallel", "arbitrary", vmem=V7X_VMEM_LIMIT_BYTES_BIG_TILES),
        name="ffn",
    )(*args)
    return out if emit_bf16 else out[0]


def _inproj_kernel(x_ref, g_ref, w_ref, cw_ref, cg_ref, cinit_ref,
                   yc_ref, u_ref, cnew_ref, carry_ref, *, parts, subs, blocks_per_seq, dc):
    i = pl.program_id(0)
    if blocks_per_seq > 1:
        @pl.when((i % blocks_per_seq) == 0)
        def _():
            carry_ref[0:2, :] = cinit_ref[0]

    w0, w1, w2 = cw_ref[0:1, :], cw_ref[1:2, :], cw_ref[2:3, :]
    for q0, qn in parts:
        xn = _rmsnorm(x_ref[q0:q0 + qn, :], g_ref[...]).astype(BF16)
        v = _dot(xn, w_ref[:, dc:2 * dc]) * _dot(xn, w_ref[:, 2 * dc:3 * dc])
        part_subs = [sub for sub in subs if q0 <= sub[0] < q0 + qn]
        taps = []
        for r0, ts, s in part_subs:
            row = lax.broadcasted_iota(jnp.int32, (ts, 1), 0)
            vs = v[r0 - q0:r0 - q0 + ts]
            prev = carry_ref if blocks_per_seq > 1 else cinit_ref.at[s]
            p0, p1 = prev[0:1, :], prev[1:2, :]
            vm1 = jnp.where(row == 0, p1, pltpu.roll(vs, 1, 0))
            vm2 = jnp.where(row == 0, p0, jnp.where(row == 1, p1, pltpu.roll(vs, 2, 0)))
            taps.append(w0 * vm2 + w1 * vm1 + w2 * vs)
            tail = vs[ts - 2:ts]
            cnew_ref[s] = tail
            if blocks_per_seq > 1:
                carry_ref[0:2, :] = tail
        b_gate = _dot(xn, w_ref[:, 0:dc])
        for (r0, ts, s), tap in zip(part_subs, taps):
            y = b_gate[r0 - q0:r0 - q0 + ts] * tap
            yc_ref[r0:r0 + ts, :] = _rmsnorm(y, cg_ref[...]).astype(yc_ref.dtype)
        u_ref[q0:q0 + qn, :] = _dot(xn, w_ref[:, 3 * dc:])


def _inproj(x, layer, g, w_in, conv_w, conv_g, conv_init, groups):
    m, d = x.shape
    n_seq = sum(nsq for nsq, _ in groups)
    dc = conv_w.shape[-1]
    ds = w_in.shape[-1] - 3 * dc
    if len(groups) == 1 and groups[0][1] >= INPROJ_ROW_TILE:
        seq_len = groups[0][1]
        tm = _row_tile(seq_len, INPROJ_ROW_TILE)
        ns, bps = 1, seq_len // tm
        npart = INPROJ_ROW_PARTS if tm % (INPROJ_ROW_PARTS * 2 * V7X_SUBLANES) == 0 else 1
        parts = tuple((r, tm // npart) for r in range(0, tm, tm // npart))
        subs = tuple((r, rows, 0) for r, rows in parts)
    else:
        tm, ns, bps, subs, r0 = m, n_seq, 1, [], 0
        for nsq, seq_len in groups:
            for _ in range(nsq):
                subs.append((r0, seq_len, len(subs)))
                r0 += seq_len
        subs, parts = tuple(subs), ((0, m),)
    kern = functools.partial(_inproj_kernel, parts=parts, subs=subs, blocks_per_seq=bps, dc=dc)
    per_layer = lambda a: _resident((None,) + a.shape[1:], lambda i: (layer, 0, 0))
    return pl.pallas_call(
        kern,
        grid=(m // tm,),
        in_specs=[
            pl.BlockSpec((tm, d), lambda i: (i, 0)),
            per_layer(g), per_layer(w_in), per_layer(conv_w), per_layer(conv_g),
            pl.BlockSpec((ns, 2, dc), lambda i: (i // bps, 0, 0)),
        ],
        out_specs=[
            pl.BlockSpec((tm, dc), lambda i: (i, 0)),
            pl.BlockSpec((tm, ds), lambda i: (i, 0)),
            pl.BlockSpec((ns, 2, dc), lambda i: (i // bps, 0, 0)),
        ],
        out_shape=[
            jax.ShapeDtypeStruct((m, dc), BF16),
            jax.ShapeDtypeStruct((m, ds), F32),
            jax.ShapeDtypeStruct((n_seq, 2, dc), F32),
        ],
        scratch_shapes=[pltpu.VMEM((V7X_SUBLANES, dc), F32)],
        compiler_params=_params("arbitrary", vmem=V7X_VMEM_LIMIT_BYTES_BIG_TILES),
        name="inproj",
    )(x, g, w_in, conv_w, conv_g, conv_init)


def _cmul(ar, ai, br, bi):
    return ar * br - ai * bi, ar * bi + ai * br


def _scan_short(al_ref, init_ref, fin_ref, sl_ref, e_ref, *, npair, groups):
    for q in range(npair):
        a_r, a_i = al_ref[q, 0:1, :], al_ref[q, 1:2, :]
        for off, nb, nckb, s0 in groups:
            e_r, e_i = init_ref[q, 0, s0:s0 + nb, :], init_ref[q, 1, s0:s0 + nb, :]
            for c in range(nckb):
                rows = pl.ds(off + c, nb, stride=nckb)
                e_ref[2 * q, rows, :] = e_r
                e_ref[2 * q + 1, rows, :] = e_i
                e_r, e_i = _cmul(a_r, a_i, e_r, e_i)
                e_r, e_i = e_r + sl_ref[2 * q, rows, :], e_i + sl_ref[2 * q + 1, rows, :]
            fin_ref[q, 0, s0:s0 + nb, :] = e_r
            fin_ref[q, 1, s0:s0 + nb, :] = e_i


def _scan_tables(q, al_ref):
    n, sub = V7X_LANES, V7X_SUBLANES
    row8 = lax.broadcasted_iota(jnp.int32, (sub, n), 0)
    a_r, a_i = al_ref[q, 0:1, :], al_ref[q, 1:2, :]
    pw = [(jnp.ones_like(a_r), jnp.zeros_like(a_r)), (a_r, a_i)]
    for k in range(2, sub + 1):
        pw.append(_cmul(*pw[k // 2], *pw[k - k // 2]))
    levels = [(jnp.where(row8 >= k, pw[k][0], 0.0), jnp.where(row8 >= k, pw[k][1], 0.0))
              for k in (1, 2, 4)]
    ent_r = jnp.zeros((sub, n), F32)
    ent_i = jnp.zeros((sub, n), F32)
    for r in range(sub):
        ent_r = jnp.where(row8 == r, pw[r + 1][0], ent_r)
        ent_i = jnp.where(row8 == r, pw[r + 1][1], ent_i)
    return levels, ent_r, ent_i


def _scan_vregs(vregs, state, tabs, sl_ref, e_ref):
    n, sub = V7X_LANES, V7X_SUBLANES
    first = lax.broadcasted_iota(jnp.int32, (sub, n), 0) == 0
    state = list(state)
    for k in vregs:
        rows = slice(k * sub, (k + 1) * sub)
        for q, (levels, ent_r, ent_i) in enumerate(tabs):
            v_r, v_i = sl_ref[2 * q, rows, :], sl_ref[2 * q + 1, rows, :]
            for shift, (c_r, c_i) in zip((1, 2, 4), levels):
                s_r, s_i = pltpu.roll(v_r, shift, 0), pltpu.roll(v_i, shift, 0)
                v_r, v_i = v_r + (c_r * s_r - c_i * s_i), v_i + (c_r * s_i + c_i * s_r)
            e0_r, e0_i = state[2 * q], state[2 * q + 1]
            g_r, g_i = _cmul(ent_r, ent_i, e0_r, e0_i)
            w_r, w_i = v_r + g_r, v_i + g_i
            e_ref[2 * q, rows, :] = jnp.where(first, e0_r, pltpu.roll(w_r, 1, 0))
            e_ref[2 * q + 1, rows, :] = jnp.where(first, e0_i, pltpu.roll(w_i, 1, 0))
            state[2 * q] = jnp.broadcast_to(w_r[sub - 1:sub, :], (sub, n))
            state[2 * q + 1] = jnp.broadcast_to(w_i[sub - 1:sub, :], (sub, n))
    return state


def _ssm_kernel(u_ref, toep_ref, bm_ref, cmt_ref, dcat_ref, al_ref, init_ref, y_ref, fin_ref,
                sl_ref, e_ref, yin_ref, *scan_scratch, chunk, npair, nck, groups):
    n, sub = V7X_LANES, V7X_SUBLANES
    ucat = jnp.concatenate([u_ref[pl.ds(t, nck, stride=chunk), :] for t in range(chunk)], axis=1)
    ub = ucat.astype(BF16)
    sl = _dot(ub, bm_ref[...])
    for g in range(2 * npair):
        sl_ref[g] = sl[:, g * n:(g + 1) * n]

    def in_chunk(cols):
        yin_ref[:, cols] = _dot(ub, toep_ref[:, cols]) + dcat_ref[:, cols] * ucat[:, cols]

    def state_dot(rows):
        e = jnp.concatenate([e_ref[g, rows, :] for g in range(2 * npair)], axis=1).astype(BF16)
        return _dot_nt(e, cmt_ref[...])

    def finish(rows, ys):
        y = jax.nn.gelu(yin_ref[rows, :] + ys)
        nrows = rows.stop - rows.start
        for t in range(chunk):
            y_ref[pl.ds(rows.start * chunk + t, nrows, stride=chunk), :] = y[:, t * n:(t + 1) * n]

    if groups is not None:
        in_chunk(slice(0, chunk * n))
        _scan_short(al_ref, init_ref, fin_ref, sl_ref, e_ref, npair=npair, groups=groups)
        finish(slice(0, nck), state_dot(slice(0, nck)))
        return

    st_ref, = scan_scratch
    seg = pl.program_id(2)

    @pl.when(seg == 0)
    def _():
        for q in range(npair):
            st_ref[2 * q] = jnp.broadcast_to(init_ref[q, 0:1, :], (sub, n))
            st_ref[2 * q + 1] = jnp.broadcast_to(init_ref[q, 1:2, :], (sub, n))

    tabs = [_scan_tables(q, al_ref) for q in range(npair)]
    nv = nck // sub
    parts = SSM_ROW_PARTS if nv % SSM_ROW_PARTS == 0 else 1
    spread = min(parts, 2)
    nslice = 4 if (chunk * n) % (4 * 2 * n) == 0 else spread
    cw = chunk * n // nslice
    col_slices = [slice(q * cw, (q + 1) * cw) for q in range(nslice)]
    state = [st_ref[i] for i in range(2 * npair)]
    pending = None
    for h in range(parts):
        for cols in col_slices[h * nslice // spread:(h + 1) * nslice // spread] if h < spread else []:
            in_chunk(cols)
        state = _scan_vregs(range(h * nv // parts, (h + 1) * nv // parts), state, tabs, sl_ref, e_ref)
        if pending is not None:
            finish(*pending)
        rows = slice(h * nck // parts, (h + 1) * nck // parts)
        pending = (rows, state_dot(rows))
    finish(*pending)
    for i in range(2 * npair):
        st_ref[i] = state[i]

    @pl.when(seg == pl.num_programs(2) - 1)
    def _():
        for q in range(npair):
            fin_ref[q, 0:1, :] = state[2 * q][0:1, :]
            fin_ref[q, 1:2, :] = state[2 * q + 1][0:1, :]


def _long_scan(groups):
    return len(groups) == 1 and (groups[0][1] // SSM_CHUNK) % V7X_SUBLANES == 0


def _ssm(u, tabs, layer, init, groups):
    toep, bm, cmt, dcat, al = tabs
    m, ds = u.shape
    n = V7X_LANES
    chunk = SSM_CHUNK
    nj = ds // n
    npair_all = al.shape[1]
    npair = npair_all // nj
    n_seq = sum(nsq for nsq, _ in groups)
    assert all(seq_len % chunk == 0 for _, seq_len in groups)
    if _long_scan(groups):
        seq_len = groups[0][1]
        tseg = _row_tile(seq_len, SSM_SEG_TOKENS, unit=chunk * V7X_SUBLANES)
        nseg = seq_len // tseg
        nck, scan_groups = tseg // chunk, None
        grid = (nj, n_seq, nseg)
        sem = ("parallel", "parallel", "arbitrary")
        u_spec = pl.BlockSpec((tseg, n), lambda j, b, s: (b * nseg + s, j))
        st_spec = pl.BlockSpec((None, npair, 2, n), lambda j, b, s: (b, j, 0, 0))
        st_shape = (n_seq, npair_all, 2, n)
        tab = lambda r, c: pl.BlockSpec((None, None, r, c), lambda j, b, s: (layer, j, 0, 0))
        al_spec = pl.BlockSpec((None, npair, 2, n), lambda j, b, s: (layer, j, 0, 0))
        scan_scratch = [pltpu.VMEM((2 * npair, V7X_SUBLANES, n), F32)]
    else:
        scan_groups, off, s0 = [], 0, 0
        for nsq, seq_len in groups:
            scan_groups.append((off, nsq, seq_len // chunk, s0))
            off += nsq * (seq_len // chunk)
            s0 += nsq
        nck, scan_groups = m // chunk, tuple(scan_groups)
        grid = (nj,)
        sem = ("parallel",)
        u_spec = pl.BlockSpec((m, n), lambda j: (0, j))
        st_spec = pl.BlockSpec((npair, 2, n_seq, n), lambda j: (j, 0, 0, 0))
        st_shape = (npair_all, 2, n_seq, n)
        tab = lambda r, c: pl.BlockSpec((None, None, r, c), lambda j: (layer, j, 0, 0))
        al_spec = pl.BlockSpec((None, npair, 2, n), lambda j: (layer, j, 0, 0))
        scan_scratch = []
    kern = functools.partial(_ssm_kernel, chunk=chunk, npair=npair, nck=nck, groups=scan_groups)
    return pl.pallas_call(
        kern,
        grid=grid,
        in_specs=[u_spec, tab(chunk * n, chunk * n), tab(chunk * n, 2 * npair * n),
                  tab(chunk * n, 2 * npair * n), tab(1, chunk * n), al_spec, st_spec],
        out_specs=[u_spec, st_spec],
        out_shape=[jax.ShapeDtypeStruct((m, ds), F32), jax.ShapeDtypeStruct(st_shape, F32)],
        scratch_shapes=[pltpu.VMEM((2 * npair, nck, n), F32)] * 2
        + [pltpu.VMEM((nck, chunk * n), F32)] + scan_scratch,
        compiler_params=_params(*sem),
        name="ssm",
    )(u, toep, bm, cmt, dcat, al, init)


def _ssm_prep_kernel(lr_ref, li_ref, ldt_ref, br_ref, bi_ref, cr_ref, ci_ref,
                     toep_ref, bm_ref, cmt_ref, al_ref, *, chunk, h):
    n = V7X_LANES
    npair = n // h // 2
    shift = h.bit_length() - 1
    lo = lax.broadcasted_iota(jnp.int32, (1, n), 1) < n // 2
    row_group = lax.broadcasted_iota(jnp.int32, (n, 1), 0) >> shift
    col_group = lax.broadcasted_iota(jnp.int32, (1, n), 1) >> shift
    pair_mask = [((row_group == 2 * q) & lo) | ((row_group == 2 * q + 1) & ~lo) for q in range(npair)]

    lr, li = lr_ref[...], li_ref[...]
    dt = jnp.exp(ldt_ref[...])
    mag = jnp.exp(lr * dt)
    ar, ai = mag * jnp.cos(li * dt), mag * jnp.sin(li * dt)
    den = lr * lr + li * li
    qr = ((ar - 1.0) * lr + ai * li) / den
    qi = (ai * lr - (ar - 1.0) * li) / den
    bbr, bbi = _cmul(qr, qi, br_ref[...], bi_ref[...])
    cr, ci = cr_ref[...], ci_ref[...]

    pw = [(jnp.ones_like(ar), jnp.zeros_like(ar))]
    for _ in range(chunk):
        pw.append(_cmul(*pw[-1], ar, ai))
    for q in range(npair):
        even, odd = 2 * q * h, (2 * q + 1) * h
        for part in range(2):
            al_ref[q, part:part + 1, :] = jnp.where(lo, pw[chunk][part][even:even + 1, :],
                                                    pw[chunk][part][odd:odd + 1, :])

    cpow = [_cmul(cr, ci, *pw[k]) for k in range(chunk + 1)]

    bb = jnp.where(lo, bbr, bbi)
    zeros = jnp.zeros((n, n), toep_ref.dtype)
    for k in range(chunk):
        kk = _dot_nt(bb, jnp.where(lo, cpow[k][0], -cpow[k][1]), precision=lax.Precision.HIGHEST)
        kk = jnp.where(row_group == col_group, kk, 0.0).astype(toep_ref.dtype)
        for s in range(chunk - k):
            toep_ref[s * n:(s + 1) * n, (s + k) * n:(s + k + 1) * n] = kk
        for s in range(k, chunk):
            if k > 0:
                toep_ref[s * n:(s + 1) * n, (s - k) * n:(s - k + 1) * n] = zeros
    for s in range(chunk):
        rows = slice(s * n, (s + 1) * n)
        er, ei = _cmul(*pw[chunk - 1 - s], bbr, bbi)
        for q in range(npair):
            c_re, c_im = slice(2 * q * n, (2 * q + 1) * n), slice((2 * q + 1) * n, (2 * q + 2) * n)
            bm_ref[rows, c_re] = jnp.where(pair_mask[q], er, 0.0).astype(bm_ref.dtype)
            bm_ref[rows, c_im] = jnp.where(pair_mask[q], ei, 0.0).astype(bm_ref.dtype)
            cmt_ref[rows, c_re] = jnp.where(pair_mask[q], cpow[s + 1][0], 0.0).astype(cmt_ref.dtype)
            cmt_ref[rows, c_im] = jnp.where(pair_mask[q], -cpow[s + 1][1], 0.0).astype(cmt_ref.dtype)


def _ssm_prep(lam_re, lam_im, log_dt, b_re, b_im, c_re, c_im, d_skip):
    nl, g, p = lam_re.shape
    h = d_skip.shape[-1]
    n = V7X_LANES
    chunk = SSM_CHUNK
    assert 2 * p == n and n % (2 * h) == 0 and h & (h - 1) == 0 and (g * h) % n == 0
    nj = g * h // n
    npair = n // h // 2
    twice = lambda a: jnp.concatenate([a, a], axis=-1)
    per_row = lambda a: jnp.repeat(twice(a), h, axis=1)
    lr, li = per_row(lam_re), per_row(lam_im)
    ldt = per_row(jnp.broadcast_to(log_dt[:, :, None], (nl, g, p)))
    rows = lambda a: twice(a).reshape(nl, g * h, n)
    br, bi = rows(jnp.swapaxes(b_re, -1, -2)), rows(jnp.swapaxes(b_im, -1, -2))
    cr, ci = rows(c_re), rows(c_im)
    in_spec = pl.BlockSpec((None, n, n), lambda l, j: (l, j, 0))
    tab = lambda r, c: pl.BlockSpec((None, None, r, c), lambda l, j: (l, j, 0, 0))
    toep, bm, cmt, al = pl.pallas_call(
        functools.partial(_ssm_prep_kernel, chunk=chunk, h=h),
        grid=(nl, nj),
        in_specs=[in_spec] * 7,
        out_specs=[tab(chunk * n, chunk * n), tab(chunk * n, 2 * npair * n),
                   tab(chunk * n, 2 * npair * n),
                   pl.BlockSpec((None, npair, 2, n), lambda l, j: (l, j, 0, 0))],
        out_shape=[
            jax.ShapeDtypeStruct((nl, nj, chunk * n, chunk * n), BF16),
            jax.ShapeDtypeStruct((nl, nj, chunk * n, 2 * npair * n), BF16),
            jax.ShapeDtypeStruct((nl, nj, chunk * n, 2 * npair * n), BF16),
            jax.ShapeDtypeStruct((nl, g // 2, 2, n), F32),
        ],
        compiler_params=_params("parallel", "parallel"),
        name="ssm_prep",
    )(lr, li, ldt, br, bi, cr, ci)
    dcat = jnp.tile(d_skip.reshape(nl, nj, 1, n), (1, 1, 1, chunk))
    return toep, bm, cmt, dcat, al


def _mixout_kernel(x_ref, yc_ref, ys_ref, glu_ref, sg_ref, wo_ref, o_ref, *, dc):
    y = ys_ref[...]
    z = y * jax.nn.sigmoid(_dot(y.astype(BF16), glu_ref[...]))
    zn = _rmsnorm(z, sg_ref[...]).astype(BF16)
    o_ref[...] = x_ref[...] + _dot(yc_ref[...], wo_ref[0:dc, :]) + _dot(zn, wo_ref[dc:, :])


def _mixout(x, yc, ys, layer, glu_w, ssm_g, w_out):
    m, d = x.shape
    dc, ds = yc.shape[1], ys.shape[1]
    tm = _row_tile(m, ROW_TILE)
    per_layer = lambda a: _resident((None,) + a.shape[1:], lambda i: (layer, 0, 0))
    return pl.pallas_call(
        functools.partial(_mixout_kernel, dc=dc),
        grid=(m // tm,),
        in_specs=[
            pl.BlockSpec((tm, d), lambda i: (i, 0)),
            pl.BlockSpec((tm, dc), lambda i: (i, 0)),
            pl.BlockSpec((tm, ds), lambda i: (i, 0)),
            per_layer(glu_w), per_layer(ssm_g), per_layer(w_out),
        ],
        out_specs=pl.BlockSpec((tm, d), lambda i: (i, 0)),
        out_shape=jax.ShapeDtypeStruct((m, d), F32),
        compiler_params=_params("parallel"),
        name="mixout",
    )(x, yc, ys, glu_w, ssm_g, w_out)


def _trunk(x, groups, conv_init, ssm_init, w, tabs, final_gain, ffn_bf16=None):
    depth = len(conv_init)
    emit = ffn_bf16 is None
    conv_out, ssm_out, casts = [], [], []
    for l in range(depth):
        fg = final_gain if l == depth - 1 else None
        if emit:
            x, *w1 = _ffn(x, w["ffn1_norm"], l, w["ffn1_w_gate"], w["ffn1_w_up"], w["ffn1_w_down"], l,
                          emit_bf16=True)
        else:
            x = _ffn(x, w["ffn1_norm"], l, *ffn_bf16[l][0], 0)
        yc, u, conv_new = _inproj(x, l, w["mix_norm"], w["w_in"], w["conv_w"], w["conv_out_norm"],
                                  conv_init[l], groups)
        ys, fin = _ssm(u, tabs, l, ssm_init[l], groups)
        x = _mixout(x, yc, ys, l, w["ssm_glu_w"], w["ssm_out_norm"], w["w_out"])
        if emit:
            x, *w2 = _ffn(x, w["ffn2_norm"], l, w["ffn2_w_gate"], w["ffn2_w_up"], w["ffn2_w_down"], l,
                          final_gain=fg, emit_bf16=True)
            casts.append((w1, w2))
        else:
            x = _ffn(x, w["ffn2_norm"], l, *ffn_bf16[l][1], 0, final_gain=fg)
        conv_out.append(conv_new)
        ssm_out.append(fin)
    return x, conv_out, ssm_out, casts


def kernel(x_prompt, x_sample, cache_conv, state_ssm_re, state_ssm_im, meta_tokens, ffn1_norm, ffn1_w_gate, ffn1_w_up, ffn1_w_down, mix_norm, w_in, conv_w, conv_out_norm, ssm_lambda_re, ssm_lambda_im, ssm_log_dt, ssm_b_re, ssm_b_im, ssm_c_re, ssm_c_im, ssm_d, ssm_glu_w, ssm_out_norm, w_out, ffn2_norm, ffn2_w_gate, ffn2_w_up, ffn2_w_down, final_norm):
    bsz, seq, d = x_prompt.shape
    dec_b, dec_t, _ = x_sample.shape
    n_meta = meta_tokens.shape[0]
    depth, g, p = ssm_lambda_re.shape
    dc = conv_w.shape[-1]
    n = V7X_LANES
    prompt_groups = ((bsz, seq),)
    small_groups = ((dec_b, dec_t), (1, n_meta))
    assert _long_scan(prompt_groups)

    tabs = _ssm_prep(ssm_lambda_re, ssm_lambda_im, ssm_log_dt, ssm_b_re, ssm_b_im,
                     ssm_c_re, ssm_c_im, ssm_d)
    cast = lambda a: a.astype(BF16)
    gain = lambda a: a[:, None, :]
    w = dict(
        ffn1_norm=gain(ffn1_norm), ffn1_w_gate=ffn1_w_gate, ffn1_w_up=ffn1_w_up,
        ffn1_w_down=ffn1_w_down, mix_norm=gain(mix_norm), w_in=cast(w_in), conv_w=conv_w,
        conv_out_norm=gain(conv_out_norm), ssm_glu_w=cast(ssm_glu_w),
        ssm_out_norm=gain(ssm_out_norm), w_out=cast(w_out), ffn2_norm=gain(ffn2_norm),
        ffn2_w_gate=ffn2_w_gate, ffn2_w_up=ffn2_w_up, ffn2_w_down=ffn2_w_down)
    final_gain = final_norm.reshape(1, d)

    n_dec = dec_b * dec_t
    x_small = jnp.concatenate([x_sample.reshape(n_dec, d), meta_tokens.astype(F32)], axis=0)
    pairs = lambda a: a.reshape(depth, -1, g // 2, n)
    s_init = jnp.transpose(jnp.stack([pairs(state_ssm_re), pairs(state_ssm_im)], axis=3),
                           (0, 2, 3, 1, 4))
    s_init = jnp.concatenate([s_init, jnp.zeros((depth, g // 2, 2, 1, n), F32)], axis=3)
    c_init = jnp.concatenate([cache_conv, jnp.zeros((depth, 1, 2, dc), F32)], axis=1)
    x_small, conv_sm, ssm_sm, ffn_bf16 = _trunk(
        x_small, small_groups, [c_init[l] for l in range(depth)],
        [s_init[l] for l in range(depth)], w, tabs, final_gain)

    conv_p0 = [jnp.broadcast_to(c[dec_b:], (bsz, 2, dc)) for c in conv_sm]
    ssm_p0 = [jnp.broadcast_to(s[None, :, :, dec_b], (bsz, g // 2, 2, n)) for s in ssm_sm]
    xp, conv_p, ssm_p, _ = _trunk(x_prompt.reshape(bsz * seq, d), prompt_groups, conv_p0, ssm_p0, w,
                                  tabs, final_gain, ffn_bf16)

    sp = jnp.stack(ssm_p)
    ss = jnp.transpose(jnp.stack(ssm_sm)[:, :, :, :dec_b], (0, 3, 1, 2, 4))
    part = lambda s, i: s[:, :, :, i, :].reshape(depth, -1, g, p)
    return (xp.reshape(bsz, seq, d), x_small[:n_dec].reshape(dec_b, dec_t, d), jnp.stack(conv_p),
            part(sp, 0), part(sp, 1), jnp.stack(conv_sm)[:, :dec_b], part(ss, 0), part(ss, 1))
```

```python
import functools

import jax
import jax.numpy as jnp
from jax import lax
from jax.experimental import pallas as pl
from jax.experimental.pallas import tpu as pltpu

F32 = jnp.float32
BF16 = jnp.bfloat16
EPS = 1e-6

V7X_LANES = 128
V7X_SUBLANES = 8
V7X_VMEM_LIMIT_BYTES = 56 * 1024 * 1024
V7X_VMEM_LIMIT_BYTES_BIG_TILES = 60 * 1024 * 1024

ROW_TILE = 512
INPROJ_ROW_TILE = 1024
INPROJ_ROW_PARTS = 2
FFN_ROW_TILE = 1024
FF_TILE = 512
FFN_EDGE_ROWS = 128
SSM_CHUNK = V7X_SUBLANES
SSM_SEG_TOKENS = 8192
SSM_ROW_PARTS = 4


def _row_tile(m, target, unit=V7X_SUBLANES):
    if m <= target:
        return m
    t = (target // unit) * unit
    while t > unit and m % t:
        t -= unit
    assert m % t == 0, (m, target)
    return t


def _lane_tile(n, target):
    t = (min(n, target) // V7X_LANES) * V7X_LANES
    while t > V7X_LANES and n % t:
        t -= V7X_LANES
    assert t > 0 and n % t == 0, (n, target)
    return t


def _params(*sem, vmem=V7X_VMEM_LIMIT_BYTES):
    return pltpu.CompilerParams(dimension_semantics=sem, vmem_limit_bytes=vmem)


def _resident(block_shape, index_map):
    return pl.BlockSpec(block_shape, index_map, pipeline_mode=pl.Buffered(1))


def _rmsnorm(x, g):
    ms = jnp.mean(x * x, axis=-1, keepdims=True)
    return x * lax.rsqrt(ms + EPS) * g


def _dot(a, b):
    return jnp.dot(a, b, preferred_element_type=F32)


def _dot_nt(a, b, precision=None):
    return lax.dot_general(a, b, (((1,), (1,)), ((), ())), preferred_element_type=F32,
                           precision=precision)


def _ffn_kernel(x_ref, g_ref, wg_ref, wu_ref, wd_ref, *rest, final_norm, emit_bf16):
    rest = list(rest)
    gf_ref = rest.pop(0) if final_norm else None
    o_ref = rest.pop(0)
    casts = [rest.pop(0) for _ in range(3)] if emit_bf16 else None
    xn_ref, = rest
    j = pl.program_id(1)
    tm = x_ref.shape[0]
    edge = _row_tile(tm, FFN_EDGE_ROWS, unit=2 * V7X_SUBLANES)

    def for_sub_blocks(fn, rolled=False):
        if rolled:
            def body(r, carry):
                fn(pl.ds(pl.multiple_of(r * edge, edge), edge))
                return carry
            lax.fori_loop(0, tm // edge, body, 0)
        else:
            for r in range(0, tm, edge):
                fn(pl.ds(r, edge))

    def step(first):
        xn = xn_ref[...]
        wg, wu = wg_ref[...].astype(BF16), wu_ref[...].astype(BF16)
        wd = (wd_ref[...] * 0.5 if emit_bf16 else wd_ref[...]).astype(BF16)
        if emit_bf16:
            for dst, w in zip(casts, (wg, wu, wd)):
                dst[...] = w
        h = (jax.nn.silu(_dot(xn, wg)) * _dot(xn, wu)).astype(BF16)
        if first:
            o_ref[...] = x_ref[...] + _dot(h, wd)
        else:
            o_ref[...] += _dot(h, wd)

    @pl.when(j == 0)
    def _():
        def norm_in(rows):
            xn_ref[rows, :] = _rmsnorm(x_ref[rows, :], g_ref[...]).astype(BF16)
        for_sub_blocks(norm_in)
        step(True)

    @pl.when(j > 0)
    def _():
        step(False)

    if final_norm:
        @pl.when(j == pl.num_programs(1) - 1)
        def _():
            def last_norm(rows):
                o_ref[rows, :] = _rmsnorm(o_ref[rows, :], gf_ref[...])
            for_sub_blocks(last_norm)


def _ffn(x, g, layer, wg, wu, wd, w_layer, final_gain=None, emit_bf16=False):
    m, d = x.shape
    f = wg.shape[-1]
    tm = _row_tile(m, FFN_ROW_TILE)
    tf = _lane_tile(f, FF_TILE)
    in_specs = [
        pl.BlockSpec((tm, d), lambda i, j: (i, 0)),
        pl.BlockSpec((None, 1, d), lambda i, j: (layer, 0, 0)),
        pl.BlockSpec((None, d, tf), lambda i, j: (w_layer, 0, j)),
        pl.BlockSpec((None, d, tf), lambda i, j: (w_layer, 0, j)),
        pl.BlockSpec((None, tf, d), lambda i, j: (w_layer, j, 0)),
    ]
    args = [x, g, wg, wu, wd]
    if final_gain is not None:
        in_specs.append(pl.BlockSpec((1, d), lambda i, j: (0, 0)))
        args.append(final_gain)
    out_specs = [pl.BlockSpec((tm, d), lambda i, j: (i, 0))]
    out_shape = [jax.ShapeDtypeStruct((m, d), F32)]
    if emit_bf16:
        assert m == tm, "each weight tile must be visited once"
        out_specs += [pl.BlockSpec((None, d, tf), lambda i, j: (0, 0, j))] * 2
        out_specs += [pl.BlockSpec((None, tf, d), lambda i, j: (0, j, 0))]
        out_shape += [jax.ShapeDtypeStruct((1, d, f), BF16)] * 2 + [jax.ShapeDtypeStruct((1, f, d), BF16)]
    out = pl.pallas_call(
        functools.partial(_ffn_kernel, final_norm=final_gain is not None, emit_bf16=emit_bf16),
        grid=(m // tm, f // tf),
        in_specs=in_specs,
        out_specs=out_specs,
        out_shape=out_shape,
        scratch_shapes=[pltpu.VMEM((tm, d), BF16)],
        compiler_params=_params("parallel", "arbitrary", vmem=V7X_VMEM_LIMIT_BYTES_BIG_TILES),
        name="ffn",
    )(*args)
    return out if emit_bf16 else out[0]


def _inproj_kernel(x_ref, g_ref, w_ref, cw_ref, cg_ref, cinit_ref,
                   yc_ref, u_ref, cnew_ref, carry_ref, *, parts, subs, blocks_per_seq, dc):
    i = pl.program_id(0)
    if blocks_per_seq > 1:
        @pl.when((i % blocks_per_seq) == 0)
        def _():
            carry_ref[0:2, :] = cinit_ref[0]

    w0, w1, w2 = cw_ref[0:1, :], cw_ref[1:2, :], cw_ref[2:3, :]
    for q0, qn in parts:
        xn = _rmsnorm(x_ref[q0:q0 + qn, :], g_ref[...]).astype(BF16)
        v = _dot(xn, w_ref[:, dc:2 * dc]) * _dot(xn, w_ref[:, 2 * dc:3 * dc])
        part_subs = [sub for sub in subs if q0 <= sub[0] < q0 + qn]
        taps = []
        for r0, ts, s in part_subs:
            row = lax.broadcasted_iota(jnp.int32, (ts, 1), 0)
            vs = v[r0 - q0:r0 - q0 + ts]
            prev = carry_ref if blocks_per_seq > 1 else cinit_ref.at[s]
            p0, p1 = prev[0:1, :], prev[1:2, :]
            vm1 = jnp.where(row == 0, p1, pltpu.roll(vs, 1, 0))
            vm2 = jnp.where(row == 0, p0, jnp.where(row == 1, p1, pltpu.roll(vs, 2, 0)))
            taps.append(w0 * vm2 + w1 * vm1 + w2 * vs)
            tail = vs[ts - 2:ts]
            cnew_ref[s] = tail
            if blocks_per_seq > 1:
                carry_ref[0:2, :] = tail
        b_gate = _dot(xn, w_ref[:, 0:dc])
        for (r0, ts, s), tap in zip(part_subs, taps):
            y = b_gate[r0 - q0:r0 - q0 + ts] * tap
            yc_ref[r0:r0 + ts, :] = _rmsnorm(y, cg_ref[...]).astype(yc_ref.dtype)
        u_ref[q0:q0 + qn, :] = _dot(xn, w_ref[:, 3 * dc:])


def _inproj(x, layer, g, w_in, conv_w, conv_g, conv_init, groups):
    m, d = x.shape
    n_seq = sum(nsq for nsq, _ in groups)
    dc = conv_w.shape[-1]
    ds = w_in.shape[-1] - 3 * dc
    if len(groups) == 1 and groups[0][1] >= INPROJ_ROW_TILE:
        seq_len = groups[0][1]
        tm = _row_tile(seq_len, INPROJ_ROW_TILE)
        ns, bps = 1, seq_len // tm
        npart = INPROJ_ROW_PARTS if tm % (INPROJ_ROW_PARTS * 2 * V7X_SUBLANES) == 0 else 1
        parts = tuple((r, tm // npart) for r in range(0, tm, tm // npart))
        subs = tuple((r, rows, 0) for r, rows in parts)
    else:
        tm, ns, bps, subs, r0 = m, n_seq, 1, [], 0
        for nsq, seq_len in groups:
            for _ in range(nsq):
                subs.append((r0, seq_len, len(subs)))
                r0 += seq_len
        subs, parts = tuple(subs), ((0, m),)
    kern = functools.partial(_inproj_kernel, parts=parts, subs=subs, blocks_per_seq=bps, dc=dc)
    per_layer = lambda a: _resident((None,) + a.shape[1:], lambda i: (layer, 0, 0))
    return pl.pallas_call(
        kern,
        grid=(m // tm,),
        in_specs=[
            pl.BlockSpec((tm, d), lambda i: (i, 0)),
            per_layer(g), per_layer(w_in), per_layer(conv_w), per_layer(conv_g),
            pl.BlockSpec((ns, 2, dc), lambda i: (i // bps, 0, 0)),
        ],
        out_specs=[
            pl.BlockSpec((tm, dc), lambda i: (i, 0)),
            pl.BlockSpec((tm, ds), lambda i: (i, 0)),
            pl.BlockSpec((ns, 2, dc), lambda i: (i // bps, 0, 0)),
        ],
        out_shape=[
            jax.ShapeDtypeStruct((m, dc), BF16),
            jax.ShapeDtypeStruct((m, ds), F32),
            jax.ShapeDtypeStruct((n_seq, 2, dc), F32),
        ],
        scratch_shapes=[pltpu.VMEM((V7X_SUBLANES, dc), F32)],
        compiler_params=_params("arbitrary", vmem=V7X_VMEM_LIMIT_BYTES_BIG_TILES),
        name="inproj",
    )(x, g, w_in, conv_w, conv_g, conv_init)


def _cmul(ar, ai, br, bi):
    return ar * br - ai * bi, ar * bi + ai * br


def _scan_short(al_ref, init_ref, fin_ref, sl_ref, e_ref, *, npair, groups):
    for q in range(npair):
        a_r, a_i = al_ref[q, 0:1, :], al_ref[q, 1:2, :]
        for off, nb, nckb, s0 in groups:
            e_r, e_i = init_ref[q, 0, s0:s0 + nb, :], init_ref[q, 1, s0:s0 + nb, :]
            for c in range(nckb):
                rows = pl.ds(off + c, nb, stride=nckb)
                e_ref[2 * q, rows, :] = e_r
                e_ref[2 * q + 1, rows, :] = e_i
                e_r, e_i = _cmul(a_r, a_i, e_r, e_i)
                e_r, e_i = e_r + sl_ref[2 * q, rows, :], e_i + sl_ref[2 * q + 1, rows, :]
            fin_ref[q, 0, s0:s0 + nb, :] = e_r
            fin_ref[q, 1, s0:s0 + nb, :] = e_i


def _scan_tables(q, al_ref):
    n, sub = V7X_LANES, V7X_SUBLANES
    row8 = lax.broadcasted_iota(jnp.int32, (sub, n), 0)
    a_r, a_i = al_ref[q, 0:1, :], al_ref[q, 1:2, :]
    pw = [(jnp.ones_like(a_r), jnp.zeros_like(a_r)), (a_r, a_i)]
    for k in range(2, sub + 1):
        pw.append(_cmul(*pw[k // 2], *pw[k - k // 2]))
    levels = [(jnp.where(row8 >= k, pw[k][0], 0.0), jnp.where(row8 >= k, pw[k][1], 0.0))
              for k in (1, 2, 4)]
    ent_r = jnp.zeros((sub, n), F32)
    ent_i = jnp.zeros((sub, n), F32)
    for r in range(sub):
        ent_r = jnp.where(row8 == r, pw[r + 1][0], ent_r)
        ent_i = jnp.where(row8 == r, pw[r + 1][1], ent_i)
    return levels, ent_r, ent_i


def _scan_vregs(vregs, state, tabs, sl_ref, e_ref):
    n, sub = V7X_LANES, V7X_SUBLANES
    first = lax.broadcasted_iota(jnp.int32, (sub, n), 0) == 0
    state = list(state)
    for k in vregs:
        rows = slice(k * sub, (k + 1) * sub)
        for q, (levels, ent_r, ent_i) in enumerate(tabs):
            v_r, v_i = sl_ref[2 * q, rows, :], sl_ref[2 * q + 1, rows, :]
            for shift, (c_r, c_i) in zip((1, 2, 4), levels):
                s_r, s_i = pltpu.roll(v_r, shift, 0), pltpu.roll(v_i, shift, 0)
                v_r, v_i = v_r + (c_r * s_r - c_i * s_i), v_i + (c_r * s_i + c_i * s_r)
            e0_r, e0_i = state[2 * q], state[2 * q + 1]
            g_r, g_i = _cmul(ent_r, ent_i, e0_r, e0_i)
            w_r, w_i = v_r + g_r, v_i + g_i
            e_ref[2 * q, rows, :] = jnp.where(first, e0_r, pltpu.roll(w_r, 1, 0))
            e_ref[2 * q + 1, rows, :] = jnp.where(first, e0_i, pltpu.roll(w_i, 1, 0))
            state[2 * q] = jnp.broadcast_to(w_r[sub - 1:sub, :], (sub, n))
            state[2 * q + 1] = jnp.broadcast_to(w_i[sub - 1:sub, :], (sub, n))
    return state


def _ssm_kernel(u_ref, toep_ref, bm_ref, cmt_ref, dcat_ref, al_ref, init_ref, y_ref, fin_ref,
                sl_ref, e_ref, yin_ref, *scan_scratch, chunk, npair, nck, groups):
    n, sub = V7X_LANES, V7X_SUBLANES
    ucat = jnp.concatenate([u_ref[pl.ds(t, nck, stride=chunk), :] for t in range(chunk)], axis=1)
    ub = ucat.astype(BF16)
    sl = _dot(ub, bm_ref[...])
    for g in range(2 * npair):
        sl_ref[g] = sl[:, g * n:(g + 1) * n]

    def in_chunk(cols):
        yin_ref[:, cols] = _dot(ub, toep_ref[:, cols]) + dcat_ref[:, cols] * ucat[:, cols]

    def state_dot(rows):
        e = jnp.concatenate([e_ref[g, rows, :] for g in range(2 * npair)], axis=1).astype(BF16)
        return _dot_nt(e, cmt_ref[...])

    def finish(rows, ys):
        y = jax.nn.gelu(yin_ref[rows, :] + ys)
        nrows = rows.stop - rows.start
        for t in range(chunk):
            y_ref[pl.ds(rows.start * chunk + t, nrows, stride=chunk), :] = y[:, t * n:(t + 1) * n]

    if groups is not None:
        in_chunk(slice(0, chunk * n))
        _scan_short(al_ref, init_ref, fin_ref, sl_ref, e_ref, npair=npair, groups=groups)
        finish(slice(0, nck), state_dot(slice(0, nck)))
        return

    st_ref, = scan_scratch
    seg = pl.program_id(2)

    @pl.when(seg == 0)
    def _():
        for q in range(npair):
            st_ref[2 * q] = jnp.broadcast_to(init_ref[q, 0:1, :], (sub, n))
            st_ref[2 * q + 1] = jnp.broadcast_to(init_ref[q, 1:2, :], (sub, n))

    tabs = [_scan_tables(q, al_ref) for q in range(npair)]
    nv = nck // sub
    parts = SSM_ROW_PARTS if nv % SSM_ROW_PARTS == 0 else 1
    spread = min(parts, 2)
    nslice = 4 if (chunk * n) % (4 * 2 * n) == 0 else spread
    cw = chunk * n // nslice
    col_slices = [slice(q * cw, (q + 1) * cw) for q in range(nslice)]
    state = [st_ref[i] for i in range(2 * npair)]
    pending = None
    for h in range(parts):
        for cols in col_slices[h * nslice // spread:(h + 1) * nslice // spread] if h < spread else []:
            in_chunk(cols)
        state = _scan_vregs(range(h * nv // parts, (h + 1) * nv // parts), state, tabs, sl_ref, e_ref)
        if pending is not None:
            finish(*pending)
        rows = slice(h * nck // parts, (h + 1) * nck // parts)
        pending = (rows, state_dot(rows))
    finish(*pending)
    for i in range(2 * npair):
        st_ref[i] = state[i]

    @pl.when(seg == pl.num_programs(2) - 1)
    def _():
        for q in range(npair):
            fin_ref[q, 0:1, :] = state[2 * q][0:1, :]
            fin_ref[q, 1:2, :] = state[2 * q + 1][0:1, :]


def _long_scan(groups):
    return len(groups) == 1 and (groups[0][1] // SSM_CHUNK) % V7X_SUBLANES == 0


def _ssm(u, tabs, layer, init, groups):
    toep, bm, cmt, dcat, al = tabs
    m, ds = u.shape
    n = V7X_LANES
    chunk = SSM_CHUNK
    nj = ds // n
    npair_all = al.shape[1]
    npair = npair_all // nj
    n_seq = sum(nsq for nsq, _ in groups)
    assert all(seq_len % chunk == 0 for _, seq_len in groups)
    if _long_scan(groups):
        seq_len = groups[0][1]
        tseg = _row_tile(seq_len, SSM_SEG_TOKENS, unit=chunk * V7X_SUBLANES)
        nseg = seq_len // tseg
        nck, scan_groups = tseg // chunk, None
        grid = (nj, n_seq, nseg)
        sem = ("parallel", "parallel", "arbitrary")
        u_spec = pl.BlockSpec((tseg, n), lambda j, b, s: (b * nseg + s, j))
        st_spec = pl.BlockSpec((None, npair, 2, n), lambda j, b, s: (b, j, 0, 0))
        st_shape = (n_seq, npair_all, 2, n)
        tab = lambda r, c: pl.BlockSpec((None, None, r, c), lambda j, b, s: (layer, j, 0, 0))
        al_spec = pl.BlockSpec((None, npair, 2, n), lambda j, b, s: (layer, j, 0, 0))
        scan_scratch = [pltpu.VMEM((2 * npair, V7X_SUBLANES, n), F32)]
    else:
        scan_groups, off, s0 = [], 0, 0
        for nsq, seq_len in groups:
            scan_groups.append((off, nsq, seq_len // chunk, s0))
            off += nsq * (seq_len // chunk)
            s0 += nsq
        nck, scan_groups = m // chunk, tuple(scan_groups)
        grid = (nj,)
        sem = ("parallel",)
        u_spec = pl.BlockSpec((m, n), lambda j: (0, j))
        st_spec = pl.BlockSpec((npair, 2, n_seq, n), lambda j: (j, 0, 0, 0))
        st_shape = (npair_all, 2, n_seq, n)
        tab = lambda r, c: pl.BlockSpec((None, None, r, c), lambda j: (layer, j, 0, 0))
        al_spec = pl.BlockSpec((None, npair, 2, n), lambda j: (layer, j, 0, 0))
        scan_scratch = []
    kern = functools.partial(_ssm_kernel, chunk=chunk, npair=npair, nck=nck, groups=scan_groups)
    return pl.pallas_call(
        kern,
        grid=grid,
        in_specs=[u_spec, tab(chunk * n, chunk * n), tab(chunk * n, 2 * npair * n),
                  tab(chunk * n, 2 * npair * n), tab(1, chunk * n), al_spec, st_spec],
        out_specs=[u_spec, st_spec],
        out_shape=[jax.ShapeDtypeStruct((m, ds), F32), jax.ShapeDtypeStruct(st_shape, F32)],
        scratch_shapes=[pltpu.VMEM((2 * npair, nck, n), F32)] * 2
        + [pltpu.VMEM((nck, chunk * n), F32)] + scan_scratch,
        compiler_params=_params(*sem),
        name="ssm",
    )(u, toep, bm, cmt, dcat, al, init)


def _ssm_prep_kernel(lr_ref, li_ref, ldt_ref, br_ref, bi_ref, cr_ref, ci_ref,
                     toep_ref, bm_ref, cmt_ref, al_ref, *, chunk, h):
    n = V7X_LANES
    npair = n // h // 2
    shift = h.bit_length() - 1
    lo = lax.broadcasted_iota(jnp.int32, (1, n), 1) < n // 2
    row_group = lax.broadcasted_iota(jnp.int32, (n, 1), 0) >> shift
    col_group = lax.broadcasted_iota(jnp.int32, (1, n), 1) >> shift
    pair_mask = [((row_group == 2 * q) & lo) | ((row_group == 2 * q + 1) & ~lo) for q in range(npair)]

    lr, li = lr_ref[...], li_ref[...]
    dt = jnp.exp(ldt_ref[...])
    mag = jnp.exp(lr * dt)
    ar, ai = mag * jnp.cos(li * dt), mag * jnp.sin(li * dt)
    den = lr * lr + li * li
    qr = ((ar - 1.0) * lr + ai * li) / den
    qi = (ai * lr - (ar - 1.0) * li) / den
    bbr, bbi = _cmul(qr, qi, br_ref[...], bi_ref[...])
    cr, ci = cr_ref[...], ci_ref[...]

    pw = [(jnp.ones_like(ar), jnp.zeros_like(ar))]
    for _ in range(chunk):
        pw.append(_cmul(*pw[-1], ar, ai))
    for q in range(npair):
        even, odd = 2 * q * h, (2 * q + 1) * h
        for part in range(2):
            al_ref[q, part:part + 1, :] = jnp.where(lo, pw[chunk][part][even:even + 1, :],
                                                    pw[chunk][part][odd:odd + 1, :])

    cpow = [_cmul(cr, ci, *pw[k]) for k in range(chunk + 1)]

    bb = jnp.where(lo, bbr, bbi)
    zeros = jnp.zeros((n, n), toep_ref.dtype)
    for k in range(chunk):
        kk = _dot_nt(bb, jnp.where(lo, cpow[k][0], -cpow[k][1]), precision=lax.Precision.HIGHEST)
        kk = jnp.where(row_group == col_group, kk, 0.0).astype(toep_ref.dtype)
        for s in range(chunk - k):
            toep_ref[s * n:(s + 1) * n, (s + k) * n:(s + k + 1) * n] = kk
        for s in range(k, chunk):
            if k > 0:
                toep_ref[s * n:(s + 1) * n, (s - k) * n:(s - k + 1) * n] = zeros
    for s in range(chunk):
        rows = slice(s * n, (s + 1) * n)
        er, ei = _cmul(*pw[chunk - 1 - s], bbr, bbi)
        for q in range(npair):
            c_re, c_im = slice(2 * q * n, (2 * q + 1) * n), slice((2 * q + 1) * n, (2 * q + 2) * n)
            bm_ref[rows, c_re] = jnp.where(pair_mask[q], er, 0.0).astype(bm_ref.dtype)
            bm_ref[rows, c_im] = jnp.where(pair_mask[q], ei, 0.0).astype(bm_ref.dtype)
            cmt_ref[rows, c_re] = jnp.where(pair_mask[q], cpow[s + 1][0], 0.0).astype(cmt_ref.dtype)
            cmt_ref[rows, c_im] = jnp.where(pair_mask[q], -cpow[s + 1][1], 0.0).astype(cmt_ref.dtype)


def _ssm_prep(lam_re, lam_im, log_dt, b_re, b_im, c_re, c_im, d_skip):
    nl, g, p = lam_re.shape
    h = d_skip.shape[-1]
    n = V7X_LANES
    chunk = SSM_CHUNK
    assert 2 * p == n and n % (2 * h) == 0 and h & (h - 1) == 0 and (g * h) % n == 0
    nj = g * h // n
    npair = n // h // 2
    twice = lambda a: jnp.concatenate([a, a], axis=-1)
    per_row = lambda a: jnp.repeat(twice(a), h, axis=1)
    lr, li = per_row(lam_re), per_row(lam_im)
    ldt = per_row(jnp.broadcast_to(log_dt[:, :, None], (nl, g, p)))
    rows = lambda a: twice(a).reshape(nl, g * h, n)
    br, bi = rows(jnp.swapaxes(b_re, -1, -2)), rows(jnp.swapaxes(b_im, -1, -2))
    cr, ci = rows(c_re), rows(c_im)
    in_spec = pl.BlockSpec((None, n, n), lambda l, j: (l, j, 0))
    tab = lambda r, c: pl.BlockSpec((None, None, r, c), lambda l, j: (l, j, 0, 0))
    toep, bm, cmt, al = pl.pallas_call(
        functools.partial(_ssm_prep_kernel, chunk=chunk, h=h),
        grid=(nl, nj),
        in_specs=[in_spec] * 7,
        out_specs=[tab(chunk * n, chunk * n), tab(chunk * n, 2 * npair * n),
                   tab(chunk * n, 2 * npair * n),
                   pl.BlockSpec((None, npair, 2, n), lambda l, j: (l, j, 0, 0))],
        out_shape=[
            jax.ShapeDtypeStruct((nl, nj, chunk * n, chunk * n), BF16),
            jax.ShapeDtypeStruct((nl, nj, chunk * n, 2 * npair * n), BF16),
            jax.ShapeDtypeStruct((nl, nj, chunk * n, 2 * npair * n), BF16),
            jax.ShapeDtypeStruct((nl, g // 2, 2, n), F32),
        ],
        compiler_params=_params("parallel", "parallel"),
        name="ssm_prep",
    )(lr, li, ldt, br, bi, cr, ci)
    dcat = jnp.tile(d_skip.reshape(nl, nj, 1, n), (1, 1, 1, chunk))
    return toep, bm, cmt, dcat, al


def _mixout_kernel(x_ref, yc_ref, ys_ref, glu_ref, sg_ref, wo_ref, o_ref, *, dc):
    y = ys_ref[...]
    z = y * jax.nn.sigmoid(_dot(y.astype(BF16), glu_ref[...]))
    zn = _rmsnorm(z, sg_ref[...]).astype(BF16)
    o_ref[...] = x_ref[...] + _dot(yc_ref[...], wo_ref[0:dc, :]) + _dot(zn, wo_ref[dc:, :])


def _mixout(x, yc, ys, layer, glu_w, ssm_g, w_out):
    m, d = x.shape
    dc, ds = yc.shape[1], ys.shape[1]
    tm = _row_tile(m, ROW_TILE)
    per_layer = lambda a: _resident((None,) + a.shape[1:], lambda i: (layer, 0, 0))
    return pl.pallas_call(
        functools.partial(_mixout_kernel, dc=dc),
        grid=(m // tm,),
        in_specs=[
            pl.BlockSpec((tm, d), lambda i: (i, 0)),
            pl.BlockSpec((tm, dc), lambda i: (i, 0)),
            pl.BlockSpec((tm, ds), lambda i: (i, 0)),
            per_layer(glu_w), per_layer(ssm_g), per_layer(w_out),
        ],
        out_specs=pl.BlockSpec((tm, d), lambda i: (i, 0)),
        out_shape=jax.ShapeDtypeStruct((m, d), F32),
        compiler_params=_params("parallel"),
        name="mixout",
    )(x, yc, ys, glu_w, ssm_g, w_out)


def _trunk(x, groups, conv_init, ssm_init, w, tabs, final_gain, ffn_bf16=None):
    depth = len(conv_init)
    emit = ffn_bf16 is None
    conv_out, ssm_out, casts = [], [], []
    for l in range(depth):
        fg = final_gain if l == depth - 1 else None
        if emit:
            x, *w1 = _ffn(x, w["ffn1_norm"], l, w["ffn1_w_gate"], w["ffn1_w_up"], w["ffn1_w_down"], l,
                          emit_bf16=True)
        else:
            x = _ffn(x, w["ffn1_norm"], l, *ffn_bf16[l][0], 0)
        yc, u, conv_new = _inproj(x, l, w["mix_norm"], w["w_in"], w["conv_w"], w["conv_out_norm"],
                                  conv_init[l], groups)
        ys, fin = _ssm(u, tabs, l, ssm_init[l], groups)
        x = _mixout(x, yc, ys, l, w["ssm_glu_w"], w["ssm_out_norm"], w["w_out"])
        if emit:
            x, *w2 = _ffn(x, w["ffn2_norm"], l, w["ffn2_w_gate"], w["ffn2_w_up"], w["ffn2_w_down"], l,
                          final_gain=fg, emit_bf16=True)
            casts.append((w1, w2))
        else:
            x = _ffn(x, w["ffn2_norm"], l, *ffn_bf16[l][1], 0, final_gain=fg)
        conv_out.append(conv_new)
        ssm_out.append(fin)
    return x, conv_out, ssm_out, casts


def kernel(x_prompt, x_sample, cache_conv, state_ssm_re, state_ssm_im, meta_tokens, ffn1_norm, ffn1_w_gate, ffn1_w_up, ffn1_w_down, mix_norm, w_in, conv_w, conv_out_norm, ssm_lambda_re, ssm_lambda_im, ssm_log_dt, ssm_b_re, ssm_b_im, ssm_c_re, ssm_c_im, ssm_d, ssm_glu_w, ssm_out_norm, w_out, ffn2_norm, ffn2_w_gate, ffn2_w_up, ffn2_w_down, final_norm):
    bsz, seq, d = x_prompt.shape
    dec_b, dec_t, _ = x_sample.shape
    n_meta = meta_tokens.shape[0]
    depth, g, p = ssm_lambda_re.shape
    dc = conv_w.shape[-1]
    n = V7X_LANES
    prompt_groups = ((bsz, seq),)
    small_groups = ((dec_b, dec_t), (1, n_meta))
    assert _long_scan(prompt_groups)

    tabs = _ssm_prep(ssm_lambda_re, ssm_lambda_im, ssm_log_dt, ssm_b_re, ssm_b_im,
                     ssm_c_re, ssm_c_im, ssm_d)
    cast = lambda a: a.astype(BF16)
    gain = lambda a: a[:, None, :]
    w = dict(
        ffn1_norm=gain(ffn1_norm), ffn1_w_gate=ffn1_w_gate, ffn1_w_up=ffn1_w_up,
        ffn1_w_down=ffn1_w_down, mix_norm=gain(mix_norm), w_in=cast(w_in), conv_w=conv_w,
        conv_out_norm=gain(conv_out_norm), ssm_glu_w=cast(ssm_glu_w),
        ssm_out_norm=gain(ssm_out_norm), w_out=cast(w_out), ffn2_norm=gain(ffn2_norm),
        ffn2_w_gate=ffn2_w_gate, ffn2_w_up=ffn2_w_up, ffn2_w_down=ffn2_w_down)
    final_gain = final_norm.reshape(1, d)

    n_dec = dec_b * dec_t
    x_small = jnp.concatenate([x_sample.reshape(n_dec, d), meta_tokens.astype(F32)], axis=0)
    pairs = lambda a: a.reshape(depth, -1, g // 2, n)
    s_init = jnp.transpose(jnp.stack([pairs(state_ssm_re), pairs(state_ssm_im)], axis=3),
                           (0, 2, 3, 1, 4))
    s_init = jnp.concatenate([s_init, jnp.zeros((depth, g // 2, 2, 1, n), F32)], axis=3)
    c_init = jnp.concatenate([cache_conv, jnp.zeros((depth, 1, 2, dc), F32)], axis=1)
    x_small, conv_sm, ssm_sm, ffn_bf16 = _trunk(
        x_small, small_groups, [c_init[l] for l in range(depth)],
        [s_init[l] for l in range(depth)], w, tabs, final_gain)

    conv_p0 = [jnp.broadcast_to(c[dec_b:], (bsz, 2, dc)) for c in conv_sm]
    ssm_p0 = [jnp.broadcast_to(s[None, :, :, dec_b], (bsz, g // 2, 2, n)) for s in ssm_sm]
    xp, conv_p, ssm_p, _ = _trunk(x_prompt.reshape(bsz * seq, d), prompt_groups, conv_p0, ssm_p0, w,
                                  tabs, final_gain, ffn_bf16)

    sp = jnp.stack(ssm_p)
    ss = jnp.transpose(jnp.stack(ssm_sm)[:, :, :, :dec_b], (0, 3, 1, 2, 4))
    part = lambda s, i: s[:, :, :, i, :].reshape(depth, -1, g, p)
    return (xp.reshape(bsz, seq, d), x_small[:n_dec].reshape(dec_b, dec_t, d), jnp.stack(conv_p),
            part(sp, 0), part(sp, 1), jnp.stack(conv_sm)[:, :dec_b], part(ss, 0), part(ss, 1))
```
